```python
import math
import jax, jax.numpy as jnp
from jax import lax
import numpy as np

D_MODEL = 1024
BATCH = 4
SEQ = 8192
DEPTH = 2

HEAD_DIM = 64
D_MIX = 2 * D_MODEL
A_WIDTH = 3 * D_MIX // 8
B_WIDTH = 3 * D_MIX // 8
C_WIDTH = D_MIX // 4
A_Q_HEADS = A_WIDTH // HEAD_DIM
A_KV_HEADS = A_Q_HEADS // 4
A_WINDOW = 128
B_HEADS = B_WIDTH // HEAD_DIM
B_PATTERNS = ((128, 1), (512, 4), (2048, 16))
C_V_DIM = 2 * HEAD_DIM
C_HEADS = C_WIDTH // C_V_DIM
N_BIAS_HEADS = A_Q_HEADS + B_HEADS + C_HEADS
REL_BUCKETS = 32
REL_MAX_DIST = 2048
BLK = 128
EPS = 1e-6
NEG = -1e30
PROJ_SIZES = (A_Q_HEADS * HEAD_DIM, A_KV_HEADS * HEAD_DIM, A_KV_HEADS * HEAD_DIM,
              B_WIDTH, B_WIDTH, B_WIDTH,
              2 * C_HEADS * HEAD_DIM, 2 * C_HEADS * HEAD_DIM, C_HEADS * C_V_DIM,
              D_MIX)
PROJ_OUT = sum(PROJ_SIZES)

kernel_name = "hybrid_swa_dilated_diff_attn_block"


def rms_norm(x, g):
    xf = x.astype(jnp.float32)
    y = xf * lax.rsqrt(jnp.mean(xf * xf, axis=-1, keepdims=True) + EPS)
    return (y * g.astype(jnp.float32)).astype(x.dtype)


def rel_bucket(dist):
    n = jnp.maximum(dist, 0)
    max_exact = REL_BUCKETS // 2
    nf = jnp.maximum(n, 1).astype(jnp.float32)
    large = max_exact + (jnp.log(nf / max_exact) / math.log(REL_MAX_DIST / max_exact)
                         * (REL_BUCKETS - max_exact)).astype(jnp.int32)
    large = jnp.minimum(large, REL_BUCKETS - 1)
    return jnp.where(n < max_exact, n, large)


def banded_attention(q, k, v, table_h, max_dist, dist_scale, sink=None):
    Bn, L, Hq, dh = q.shape
    Hkv = k.shape[2]
    G = Hq // Hkv
    nb = L // BLK
    dist = jnp.arange(BLK)[:, None] + BLK - jnp.arange(2 * BLK)[None, :]
    valid = (jnp.arange(nb)[:, None] * BLK - BLK + jnp.arange(2 * BLK)[None, :]) >= 0
    allowed = ((dist >= 0) & (dist <= max_dist))[None] & valid[:, None, :]
    bias = table_h[rel_bucket(dist * dist_scale)].astype(jnp.float32)
    bias = bias.transpose(2, 0, 1).reshape(Hkv, G, BLK, 2 * BLK)

    def key_blocks(t):
        tp = jnp.pad(t, ((0, 0), (BLK, 0), (0, 0), (0, 0)))
        prev = tp[:, :L].reshape(Bn, nb, BLK, Hkv, dh)
        cur = t.reshape(Bn, nb, BLK, Hkv, dh)
        return jnp.concatenate([prev, cur], axis=2)

    kb, vb = key_blocks(k), key_blocks(v)
    qb = q.reshape(Bn, nb, BLK, Hkv, G, dh)
    s = jnp.einsum('bnqhgd,bnkhd->bnhgqk', qb, kb,
                   preferred_element_type=jnp.float32) * (1.0 / math.sqrt(dh))
    s = s + bias[None, None]
    s = jnp.where(allowed[None, :, None, None], s, NEG)
    m = jnp.max(s, axis=-1)
    if sink is not None:
        sk = sink.astype(jnp.float32).reshape(Hkv, G)[None, None, :, :, None]
        m = jnp.maximum(m, sk)
    e = jnp.exp(s - m[..., None])
    denom = jnp.sum(e, axis=-1)
    if sink is not None:
        denom = denom + jnp.exp(sk - m)
    p = e / denom[..., None]
    lse = m + jnp.log(denom)
    o = jnp.einsum('bnhgqk,bnkhd->bnqhgd', p.astype(v.dtype), vb)
    o = o.reshape(Bn, L, Hq, dh)
    lse = lse.transpose(0, 1, 4, 2, 3).reshape(Bn, L, Hq)
    return o, lse


def to_strided(t, d, Sp):
    Bn, S = t.shape[:2]
    t = jnp.pad(t, [(0, 0), (0, Sp - S)] + [(0, 0)] * (t.ndim - 2))
    t = t.reshape((Bn, Sp // d, d) + t.shape[2:])
    t = jnp.moveaxis(t, 2, 1)
    return t.reshape((Bn * d, Sp // d) + t.shape[3:])


def from_strided(t, d, Bn, S):
    Sp = t.shape[1] * d
    t = t.reshape((Bn, d, Sp // d) + t.shape[2:])
    t = jnp.moveaxis(t, 1, 2).reshape((Bn, Sp) + t.shape[3:])
    return t[:, :S]


def dilated_mixture(q, k, v, table_b):
    Bn, S = q.shape[:2]
    outs, lses = [], []
    for (w, d) in B_PATTERNS:
        span = d * BLK
        Sp = -(-S // span) * span
        o, lse = banded_attention(to_strided(q, d, Sp), to_strided(k, d, Sp), to_strided(v, d, Sp),
                                  table_b, w // d, d)
        outs.append(from_strided(o, d, Bn, S))
        lses.append(from_strided(lse, d, Bn, S))
    alpha = jax.nn.softmax(jnp.stack(lses, axis=0), axis=0)
    o = jnp.einsum('pbsh,pbshd->bshd', alpha, jnp.stack(outs, axis=0).astype(jnp.float32))
    return o.astype(q.dtype)


def diff_attention(q, k, v, table_c, lam):
    Bn, S, H, _, dh = q.shape
    nb = S // BLK
    qb = jnp.moveaxis(q.reshape(Bn, nb, BLK, H, 2, dh), 1, 0)
    kpos = jnp.arange(S)
    scale = 1.0 / math.sqrt(dh)

    def block(args):
        i, qi = args
        dist = (i * BLK + jnp.arange(BLK))[:, None] - kpos[None, :]
        bias = table_c[rel_bucket(dist)].astype(jnp.float32).transpose(2, 0, 1)
        s = jnp.einsum('bqhmd,bkhmd->bhmqk', qi, k, preferred_element_type=jnp.float32) * scale
        s = s + bias[None, :, None]
        s = jnp.where((dist >= 0)[None, None, None], s, NEG)
        p = jax.nn.softmax(s, axis=-1)
        a = p[:, :, 0] - lam * p[:, :, 1]
        return jnp.einsum('bhqk,bkhe->bqhe', a.astype(v.dtype), v)

    o = lax.map(block, (jnp.arange(nb), qb))
    return jnp.moveaxis(o, 0, 1).reshape(Bn, S, H, C_V_DIM)


def hybrid_layer(x, c_act, layer_idx, rel_table, w_in, w_out, w_ada, b_ada, g_pre, g_post,
                 a_sinks, lam_q1, lam_k1, lam_q2, lam_k2, g_sub):
    Bn, S, _ = x.shape
    mod = c_act @ w_ada + b_ada
    shift, scale, gate = jnp.split(mod, 3, axis=-1)
    h = rms_norm(x, g_pre) * (1 + scale[:, None]) + shift[:, None]
    proj = h @ w_in
    cuts, acc = [], 0
    for sz in PROJ_SIZES[:-1]:
        acc += sz
        cuts.append(acc)
    aq, ak, av, bq, bk, bv, cq, ck, cv, z = jnp.split(proj, cuts, axis=-1)

    ya, _ = banded_attention(aq.reshape(Bn, S, A_Q_HEADS, HEAD_DIM),
                             ak.reshape(Bn, S, A_KV_HEADS, HEAD_DIM),
                             av.reshape(Bn, S, A_KV_HEADS, HEAD_DIM),
                             rel_table[:, :A_Q_HEADS], A_WINDOW - 1, 1, sink=a_sinks)
    ya = ya.reshape(Bn, S, A_WIDTH)

    yb = dilated_mixture(bq.reshape(Bn, S, B_HEADS, HEAD_DIM), bk.reshape(Bn, S, B_HEADS, HEAD_DIM),
                         bv.reshape(Bn, S, B_HEADS, HEAD_DIM),
                         rel_table[:, A_Q_HEADS:A_Q_HEADS + B_HEADS])
    yb = yb.reshape(Bn, S, B_WIDTH)

    lam_init = 0.8 - 0.6 * math.exp(-0.3 * layer_idx)
    lam = (jnp.exp(jnp.sum(lam_q1.astype(jnp.float32) * lam_k1.astype(jnp.float32)))
           - jnp.exp(jnp.sum(lam_q2.astype(jnp.float32) * lam_k2.astype(jnp.float32))) + lam_init)
    yc = diff_attention(cq.reshape(Bn, S, C_HEADS, 2, HEAD_DIM), ck.reshape(Bn, S, C_HEADS, 2, HEAD_DIM),
                        cv.reshape(Bn, S, C_HEADS, C_V_DIM), rel_table[:, A_Q_HEADS + B_HEADS:], lam)
    yc = (rms_norm(yc, g_sub) * (1.0 - lam_init)).reshape(Bn, S, C_WIDTH)

    y = jnp.concatenate([ya, yb, yc], axis=-1) * jax.nn.silu(z)
    y = y @ w_out
    return x + gate[:, None] * rms_norm(y, g_post)


def setup_inputs(seed: int = 0) -> dict:
    key = jax.random.key(seed)
    ks = jax.random.split(key, 16)
    f32 = jnp.float32
    nrm = lambda k, shape, s: jax.random.normal(k, shape, f32) * s
    return {
        'x': nrm(ks[0], (BATCH, SEQ, D_MODEL), 1.0),
        'c': nrm(ks[1], (BATCH, D_MODEL), 1.0),
        'rel_table': nrm(ks[2], (REL_BUCKETS, N_BIAS_HEADS), 0.5),
        'w_in': nrm(ks[3], (DEPTH, D_MODEL, PROJ_OUT), D_MODEL ** -0.5),
        'w_out': nrm(ks[4], (DEPTH, D_MIX, D_MODEL), D_MIX ** -0.5),
        'w_ada': nrm(ks[5], (DEPTH, D_MODEL, 3 * D_MODEL), 0.5 * D_MODEL ** -0.5),
        'b_ada': nrm(ks[6], (DEPTH, 3 * D_MODEL), 0.01),
        'g_pre': 1.0 + nrm(ks[7], (DEPTH, D_MODEL), 0.05),
        'g_post': 1.0 + nrm(ks[8], (DEPTH, D_MODEL), 0.05),
        'a_sinks': nrm(ks[9], (DEPTH, A_Q_HEADS), 0.5),
        'lam_q1': nrm(ks[10], (DEPTH, HEAD_DIM), 0.1),
        'lam_k1': nrm(ks[11], (DEPTH, HEAD_DIM), 0.1),
        'lam_q2': nrm(ks[12], (DEPTH, HEAD_DIM), 0.1),
        'lam_k2': nrm(ks[13], (DEPTH, HEAD_DIM), 0.1),
        'g_sub': 1.0 + nrm(ks[14], (DEPTH, C_V_DIM), 0.05),
    }


def reference(x, c, rel_table, w_in, w_out, w_ada, b_ada, g_pre, g_post, a_sinks,
              lam_q1, lam_k1, lam_q2, lam_k2, g_sub):
    c_act = jax.nn.silu(c)
    for l in range(DEPTH):
        x = hybrid_layer(x, c_act, l, rel_table, w_in[l], w_out[l], w_ada[l], b_ada[l], g_pre[l],
                         g_post[l], a_sinks[l], lam_q1[l], lam_k1[l], lam_q2[l], lam_k2[l], g_sub[l])
    return x
```

```python
import functools
import math

import numpy as np
import jax
import jax.numpy as jnp
from jax import lax
from jax.experimental import pallas as pl
from jax.experimental.pallas import tpu as pltpu

F32 = jnp.float32
BF16 = jnp.bfloat16

HEAD_DIM = 64
LANES = 128
BLK = 128
A_WINDOW = 128
B_PATTERNS = ((128, 1), (512, 4), (2048, 16))
REL_BUCKETS = 32
REL_MAX_DIST = 2048
EPS = 1e-6
NEG = -1e30
ATT_SCALE = 1.0 / math.sqrt(HEAD_DIM)

TILE_M = 512
BAND_BLOCKS = 4
VMEM_LIMIT = 56 * 1024 * 1024


def _rel_bucket_np(n):
    n = np.maximum(np.asarray(n, np.int64), 0)
    max_exact = REL_BUCKETS // 2
    nf = np.maximum(n, 1).astype(np.float32)
    large = max_exact + (np.log(nf / np.float32(max_exact))
                         / np.float32(math.log(REL_MAX_DIST / max_exact))
                         * np.float32(REL_BUCKETS - max_exact)).astype(np.int32)
    large = np.minimum(large, REL_BUCKETS - 1)
    return np.where(n < max_exact, n, large)


_BUCKET_OF = _rel_bucket_np(np.arange(0, 1 << 18))
_THR = [0] + [int(np.argmax(_BUCKET_OF >= b)) for b in range(1, REL_BUCKETS)]
FAR_DIST = _THR[REL_BUCKETS - 1]


def _bucket_values(dist, dist_scale, lo, hi, table_ref, col):
    b_lo = int(_BUCKET_OF[max(lo, 0) * dist_scale])
    b_hi = int(_BUCKET_OF[max(hi, 0) * dist_scale])
    val = jnp.full(dist.shape, table_ref[b_lo, col], F32)
    for b in range(b_lo + 1, b_hi + 1):
        thr = -(-_THR[b] // dist_scale)
        val = jnp.where(dist >= thr, table_ref[b, col], val)
    return val


def _band_bias_kernel(table_ref, a_ref, b_ref):
    h = pl.program_id(0)
    n_a_heads = pl.num_programs(0)
    row = lax.broadcasted_iota(jnp.int32, (BLK, 2 * BLK), 0)
    col = lax.broadcasted_iota(jnp.int32, (BLK, 2 * BLK), 1)
    dist = row + BLK - col
    first_ok = col >= BLK

    def tile(max_dist, dist_scale, table_col):
        val = _bucket_values(dist, dist_scale, 0, max_dist, table_ref, table_col)
        val = jnp.where((dist >= 0) & (dist <= max_dist), val, NEG)
        return jnp.where(first_ok, val, NEG), val

    a_ref[0], a_ref[1] = tile(A_WINDOW - 1, 1, h)
    for p, (w, d) in enumerate(B_PATTERNS):
        b_ref[p, 0], b_ref[p, 1] = tile(w // d, d, n_a_heads + h)


def _diff_bias_kernel(table_ref, out_ref, *, tile, n_near, head_offset):
    h = pl.program_id(0)
    row = lax.broadcasted_iota(jnp.int32, (BLK, BLK), 0)
    col = lax.broadcasted_iota(jnp.int32, (BLK, BLK), 1)
    sub = tile // BLK
    pieces = {}
    for delta in range(-(sub - 1), n_near * sub):
        if delta < 0:
            pieces[delta] = jnp.full((BLK, BLK), NEG, F32)
            continue
        dist = row - col + delta * BLK
        val = _bucket_values(dist, 1, delta * BLK - (BLK - 1), delta * BLK + BLK - 1,
                             table_ref, head_offset + h)
        pieces[delta] = jnp.where(dist >= 0, val, NEG) if delta == 0 else val
    for dt in range(n_near):
        for a in range(sub):
            for b in range(sub):
                out_ref[dt, a * BLK:(a + 1) * BLK, b * BLK:(b + 1) * BLK] = pieces[dt * sub + a - b]


def _bias_tables(rel_table, n_a_heads, n_b_heads, n_c_heads, tile, n_near):
    smem = pl.BlockSpec(memory_space=pltpu.SMEM)
    bias_a, bias_b = pl.pallas_call(
        _band_bias_kernel,
        grid=(n_a_heads,),
        in_specs=[smem],
        out_specs=[pl.BlockSpec((2, BLK, 2 * BLK), lambda h: (0, h, 0)),
                   pl.BlockSpec((3, 2, BLK, 2 * BLK), lambda h: (0, 0, h, 0))],
        out_shape=[jax.ShapeDtypeStruct((2, n_a_heads * BLK, 2 * BLK), F32),
                   jax.ShapeDtypeStruct((3, 2, n_b_heads * BLK, 2 * BLK), F32)],
        name="band_bias",
    )(rel_table)
    bias_c = pl.pallas_call(
        functools.partial(_diff_bias_kernel, tile=tile, n_near=n_near,
                          head_offset=n_a_heads + n_b_heads),
        grid=(n_c_heads,),
        in_specs=[smem],
        out_specs=pl.BlockSpec((None, n_near, tile, tile), lambda h: (h, 0, 0, 0)),
        out_shape=jax.ShapeDtypeStruct((n_c_heads, n_near, tile, tile), F32),
        compiler_params=pltpu.CompilerParams(vmem_limit_bytes=VMEM_LIMIT),
        name="diff_bias",
    )(rel_table)
    return bias_a, bias_b, bias_c


def _split_bf16(v):
    hi = v.astype(BF16)
    lo = (v - hi.astype(F32)).astype(BF16)
    return hi, lo


def _mod_kernel(c_ref, w_ref, b_ref, out_ref):
    c = c_ref[...]
    s = c / (1.0 + jnp.exp(-c))
    s_hi, s_lo = _split_bf16(s)
    w_hi, w_lo = _split_bf16(w_ref[...])
    acc = jnp.dot(s_hi, w_hi, preferred_element_type=F32)
    acc += jnp.dot(s_lo, w_hi, preferred_element_type=F32)
    acc += jnp.dot(s_hi, w_lo, preferred_element_type=F32)
    out_ref[...] = acc + b_ref[...]


def _modulation(c, w_ada, b_ada):
    depth, d_model, n_out = w_ada.shape
    rows = 8
    c_pad = jnp.zeros((rows, d_model), F32).at[:c.shape[0]].set(c)
    tn = 768
    out = pl.pallas_call(
        _mod_kernel,
        grid=(depth, n_out // tn),
        in_specs=[pl.BlockSpec((rows, d_model), lambda l, j: (0, 0)),
                  pl.BlockSpec((None, d_model, tn), lambda l, j: (l, 0, j)),
                  pl.BlockSpec((None, 1, tn), lambda l, j: (l, 0, j))],
        out_specs=pl.BlockSpec((None, rows, tn), lambda l, j: (l, 0, j)),
        out_shape=jax.ShapeDtypeStruct((depth, rows, n_out), F32),
        name="adaln_mod",
    )(c_pad, w_ada, b_ada.reshape(depth, 1, n_out))
    return out[:, :c.shape[0]]


def _in_proj_kernel(x_ref, mod_ref, g_ref, w_ref, wkt_ref, *out_refs, widths):
    d_model = x_ref.shape[-1]
    x = x_ref[...]
    ms = jnp.mean(x * x, axis=-1, keepdims=True)
    shift = mod_ref[:, :d_model]
    scale = mod_ref[:, d_model:2 * d_model]
    h = (x * lax.rsqrt(ms + EPS)) * g_ref[...]
    h = (h * (1.0 + scale) + shift).astype(BF16)
    *col_refs, kt_ref = out_refs
    off = 0
    for ref, width in zip(col_refs, widths):
        ref[...] = jnp.dot(h, w_ref[:, off:off + width],
                           preferred_element_type=F32).astype(ref.dtype)
        off += width
    kt_ref[...] = lax.dot_general(wkt_ref[...], h, (((1,), (1,)), ((), ())),
                                  preferred_element_type=F32).astype(kt_ref.dtype)


def _in_proj(x, mod_l, g_pre_l, w_cols, w_kt, widths):
    bn, s, d_model = x.shape
    tm = TILE_M
    n_t = s // tm
    n_cols = w_cols.shape[1]
    kt_rows = w_kt.shape[0]
    out_shape = [jax.ShapeDtypeStruct((bn, s, w), BF16) for w in widths]
    out_shape.append(jax.ShapeDtypeStruct((bn, n_t, kt_rows, tm), BF16))
    out_specs = [pl.BlockSpec((None, tm, w), lambda b, i: (b, i, 0)) for w in widths]
    out_specs.append(pl.BlockSpec((None, None, kt_rows, tm), lambda b, i: (b, i, 0, 0)))
    return pl.pallas_call(
        functools.partial(_in_proj_kernel, widths=widths),
        grid=(bn, n_t),
        in_specs=[pl.BlockSpec((None, tm, d_model), lambda b, i: (b, i, 0)),
                  pl.BlockSpec((None, 1, 3 * d_model), lambda b, i: (b, 0, 0)),
                  pl.BlockSpec((1, d_model), lambda b, i: (0, 0)),
                  pl.BlockSpec((d_model, n_cols), lambda b, i: (0, 0),
                               pipeline_mode=pl.Buffered(1)),
                  pl.BlockSpec((kt_rows, d_model), lambda b, i: (0, 0),
                               pipeline_mode=pl.Buffered(1))],
        out_specs=out_specs,
        out_shape=out_shape,
        compiler_params=pltpu.CompilerParams(
            dimension_semantics=("parallel", "parallel"), vmem_limit_bytes=VMEM_LIMIT),
        name="in_proj",
    )(x, mod_l.reshape(bn, 1, 3 * d_model), g_pre_l.reshape(1, d_model), w_cols, w_kt)


def _lane_masks():
    lane = lax.broadcasted_iota(jnp.int32, (1, LANES), 1)
    even = lane < HEAD_DIM
    m_even = jnp.where(even, 1.0, 0.0).astype(BF16)
    m_odd = jnp.where(even, 0.0, 1.0).astype(BF16)
    return even, m_even, m_odd


def _banded_kernel(*refs, n_pairs, pairs_per_kv, has_sink):
    if has_sink:
        sink_ref, q_ref, kp_ref, kc_ref, vp_ref, vc_ref, bias_ref, o_ref, k_buf, v_buf = refs
        lse_ref = None
    else:
        q_ref, kp_ref, kc_ref, vp_ref, vc_ref, bias_ref, o_ref, lse_ref, k_buf, v_buf = refs
    step = pl.program_id(2)
    nblk = q_ref.shape[0] // BLK
    k_buf[:BLK] = kp_ref[...]
    k_buf[BLK:] = kc_ref[...]
    v_buf[:BLK] = vp_ref[...]
    v_buf[BLK:] = vc_ref[...]
    even, m_even, m_odd = _lane_masks()
    lane = lax.broadcasted_iota(jnp.int32, (BLK, LANES), 1)
    top_rows = lax.broadcasted_iota(jnp.int32, (2 * BLK, 1), 0) < BLK

    def block(ib, carry):
        r0 = pl.multiple_of(ib * BLK, BLK)
        variant = jnp.where((step == 0) & (ib == 0), 0, 1)
        lse_tile = jnp.zeros((BLK, LANES), F32)
        for j in range(n_pairs):
            g = j // pairs_per_kv
            q2 = q_ref[pl.ds(r0, BLK), j * LANES:(j + 1) * LANES]
            qs = jnp.concatenate([q2 * m_even, q2 * m_odd], axis=0)
            k2 = k_buf[pl.ds(r0, 2 * BLK), g * LANES:(g + 1) * LANES]
            v2 = v_buf[pl.ds(r0, 2 * BLK), g * LANES:(g + 1) * LANES]
            s = lax.dot_general(qs, k2, (((1,), (1,)), ((), ())), preferred_element_type=F32)
            s = s + bias_ref[variant, j * 2 * BLK:(j + 1) * 2 * BLK, :]
            m = jnp.max(s, axis=-1, keepdims=True)
            if has_sink:
                sk = jnp.where(top_rows, sink_ref[2 * j], sink_ref[2 * j + 1])
                m = jnp.maximum(m, sk)
            e = jnp.exp(s - m)
            den = jnp.sum(e, axis=-1, keepdims=True)
            if has_sink:
                den = den + jnp.exp(sk - m)
            o2 = jnp.dot(e.astype(BF16), v2, preferred_element_type=F32) / den
            o_ref[pl.ds(r0, BLK), j * LANES:(j + 1) * LANES] = (
                jnp.where(even, o2[:BLK], o2[BLK:]).astype(o_ref.dtype))
            if lse_ref is not None:
                lse = m + jnp.log(den)
                lse_tile = jnp.where(lane == 2 * j, lse[:BLK], lse_tile)
                lse_tile = jnp.where(lane == 2 * j + 1, lse[BLK:], lse_tile)
        if lse_ref is not None:
            lse_ref[pl.ds(r0, BLK), :] = lse_tile
        return carry

    lax.fori_loop(0, nblk, block, 0)


def _banded_attention(q, k, v, bias, dilation, pairs_per_kv, sinks=None):
    bn, s, wq = q.shape
    wkv = k.shape[-1]
    d = dilation
    sd = s // d
    nblk = min(BAND_BLOCKS, sd // BLK)
    rows = nblk * BLK
    n_steps = sd // rows
    n_pairs = wq // LANES
    qv = q.reshape(bn, sd, d * wq)
    kv_ = k.reshape(bn, sd, d * wkv)
    vv = v.reshape(bn, sd, d * wkv)
    cur = lambda b, r, n: (b, n, r)
    prev = lambda b, r, n: (b, jnp.maximum(n * nblk - 1, 0), r)
    in_specs = [pl.BlockSpec((None, rows, wq), cur),
                pl.BlockSpec((None, BLK, wkv), prev),
                pl.BlockSpec((None, rows, wkv), cur),
                pl.BlockSpec((None, BLK, wkv), prev),
                pl.BlockSpec((None, rows, wkv), cur),
                pl.BlockSpec(bias.shape, lambda b, r, n: (0, 0, 0))]
    args = [qv, kv_, kv_, vv, vv, bias]
    out_specs = [pl.BlockSpec((None, rows, wq), cur)]
    out_shape = [jax.ShapeDtypeStruct((bn, sd, d * wq), BF16)]
    has_sink = sinks is not None
    if has_sink:
        in_specs.insert(0, pl.BlockSpec(memory_space=pltpu.SMEM))
        args.insert(0, sinks)
    else:
        out_specs.append(pl.BlockSpec((None, rows, LANES), cur))
        out_shape.append(jax.ShapeDtypeStruct((bn, sd, d * LANES), F32))
    outs = pl.pallas_call(
        functools.partial(_banded_kernel, n_pairs=n_pairs, pairs_per_kv=pairs_per_kv,
                          has_sink=has_sink),
        grid=(bn, d, n_steps),
        in_specs=in_specs,
        out_specs=out_specs,
        out_shape=out_shape,
        scratch_shapes=[pltpu.VMEM((rows + BLK, wkv), BF16), pltpu.VMEM((rows + BLK, wkv), BF16)],
        compiler_params=pltpu.CompilerParams(
            dimension_semantics=("parallel", "parallel", "arbitrary"),
            vmem_limit_bytes=VMEM_LIMIT),
        name="banded_sink" if has_sink else f"banded_d{d}",
    )(*args)
    o = outs[0].reshape(bn, s, wq)
    if has_sink:
        return o
    return o, outs[1].reshape(bn, s, LANES)


def _diff_kernel(far_ref, q_ref, kt_ref, v_ref, bias_ref, lam_ref, g_ref, o_ref,
                 qs_s, m_s, l_s, acc_s, *, lam_init, n_near):
    h = pl.program_id(1)
    i = pl.program_id(2)
    t = q_ref.shape[0]
    _, m_even, m_odd = _lane_masks()
    q2 = q_ref[...]
    qs_s[:t] = q2 * m_even
    qs_s[t:] = q2 * m_odd
    m_s[...] = jnp.full(m_s.shape, NEG, F32)
    l_s[...] = jnp.zeros(l_s.shape, F32)
    acc_s[...] = jnp.zeros(acc_s.shape, F32)
    far_bias = far_ref[h]

    def update(j, s, row_max, shift):
        m_old = m_s[...]
        m_new = jnp.maximum(m_old, row_max)
        p = jnp.exp(s - (m_new - shift))
        alpha = jnp.exp(m_old - m_new)
        l_s[...] = alpha * l_s[...] + jnp.sum(p, axis=-1, keepdims=True)
        k0 = pl.multiple_of(j * t, t)
        acc_s[...] = alpha * acc_s[...] + jnp.dot(p.astype(BF16), v_ref[pl.ds(k0, t), :],
                                                  preferred_element_type=F32)
        m_s[...] = m_new

    def far_tile(j, carry):
        s = jnp.dot(qs_s[...], kt_ref[j], preferred_element_type=F32)
        update(j, s, jnp.max(s, axis=-1, keepdims=True) + far_bias, far_bias)
        return carry

    def near_tile(j, carry):
        s = jnp.dot(qs_s[...], kt_ref[j], preferred_element_type=F32)
        b = bias_ref[i - j]
        s = s + jnp.concatenate([b, b], axis=0)
        update(j, s, jnp.max(s, axis=-1, keepdims=True), 0.0)
        return carry

    n_far = jnp.maximum(i - (n_near - 1), 0)
    lax.fori_loop(0, n_far, far_tile, 0)
    lax.fori_loop(n_far, i + 1, near_tile, 0)

    o = acc_s[...] / l_s[...]
    lam = (jnp.exp(jnp.sum(lam_ref[0:1, :] * lam_ref[1:2, :], axis=-1, keepdims=True))
           - jnp.exp(jnp.sum(lam_ref[2:3, :] * lam_ref[3:4, :], axis=-1, keepdims=True)) + lam_init)
    y = o[:t] - lam * o[t:]
    y = y * lax.rsqrt(jnp.mean(y * y, axis=-1, keepdims=True) + EPS)
    o_ref[...] = (y * g_ref[...] * (1.0 - lam_init)).astype(o_ref.dtype)


def _diff_attention(q, kt, v, bias_c, far_bias, lam_params, g_sub_l, lam_init):
    bn, s, w = q.shape
    t = TILE_M
    n_t = s // t
    n_heads = w // LANES
    n_near = bias_c.shape[1]
    return pl.pallas_call(
        functools.partial(_diff_kernel, lam_init=lam_init, n_near=n_near),
        grid=(bn, n_heads, n_t),
        in_specs=[pl.BlockSpec(memory_space=pltpu.SMEM),
                  pl.BlockSpec((None, t, LANES), lambda b, h, i: (b, i, h)),
                  pl.BlockSpec((None, n_t, LANES, t), lambda b, h, i: (b, 0, h, 0)),
                  pl.BlockSpec((None, s, LANES), lambda b, h, i: (b, 0, h)),
                  pl.BlockSpec((None, n_near, t, t), lambda b, h, i: (h, 0, 0, 0)),
                  pl.BlockSpec((4, HEAD_DIM), lambda b, h, i: (0, 0)),
                  pl.BlockSpec((1, LANES), lambda b, h, i: (0, 0))],
        out_specs=pl.BlockSpec((None, t, LANES), lambda b, h, i: (b, i, h)),
        out_shape=jax.ShapeDtypeStruct((bn, s, w), BF16),
        scratch_shapes=[pltpu.VMEM((2 * t, LANES), BF16),
                        pltpu.VMEM((2 * t, 1), F32),
                        pltpu.VMEM((2 * t, 1), F32),
                        pltpu.VMEM((2 * t, LANES), F32)],
        compiler_params=pltpu.CompilerParams(
            dimension_semantics=("parallel", "parallel", "arbitrary"),
            vmem_limit_bytes=VMEM_LIMIT),
        name="diff_attn",
    )(far_bias, q, kt, v, bias_c, lam_params, g_sub_l.reshape(1, LANES))


def _out_proj_kernel(x_ref, mod_ref, g_ref, w_ref, e_ref, ya_ref, o1_ref, o2_ref, o3_ref,
                     l1_ref, l2_ref, l3_ref, yc_ref, z_ref, out_ref):
    d_model = x_ref.shape[-1]
    wa = ya_ref.shape[-1]
    wb = o1_ref.shape[-1]
    z = z_ref[...].astype(F32)
    sz = z / (1.0 + jnp.exp(-z))

    lses = [l1_ref[...], l2_ref[...], l3_ref[...]]
    mx = jnp.maximum(jnp.maximum(lses[0], lses[1]), lses[2])
    ws = [jnp.exp(l - mx) for l in lses]
    tot = ws[0] + ws[1] + ws[2]
    yb = jnp.zeros((x_ref.shape[0], wb), F32)
    for wgt, o_ref in zip(ws, (o1_ref, o2_ref, o3_ref)):
        a_hi, a_lo = _split_bf16(wgt / tot)
        spread = (jnp.dot(a_hi, e_ref[...], preferred_element_type=F32)
                  + jnp.dot(a_lo, e_ref[...], preferred_element_type=F32))
        yb = yb + spread * o_ref[...].astype(F32)

    ga = (ya_ref[...].astype(F32) * sz[:, :wa]).astype(BF16)
    gb = (yb * sz[:, wa:wa + wb]).astype(BF16)
    gc = (yc_ref[...].astype(F32) * sz[:, wa + wb:]).astype(BF16)
    y = jnp.dot(ga, w_ref[:wa], preferred_element_type=F32)
    y += jnp.dot(gb, w_ref[wa:wa + wb], preferred_element_type=F32)
    y += jnp.dot(gc, w_ref[wa + wb:], preferred_element_type=F32)
    r = (y * lax.rsqrt(jnp.mean(y * y, axis=-1, keepdims=True) + EPS)) * g_ref[...]
    gate = mod_ref[:, 2 * d_model:]
    out_ref[...] = x_ref[...] + gate * r


def _out_proj(x, mod_l, g_post_l, w_out_l, spread, ya, ob, lb, yc, z):
    bn, s, d_model = x.shape
    tm = TILE_M
    row = lambda w: pl.BlockSpec((None, tm, w), lambda b, i: (b, i, 0))
    whole = lambda a: pl.BlockSpec(a.shape, lambda b, i: (0,) * a.ndim,
                                   pipeline_mode=pl.Buffered(1))
    g2 = g_post_l.reshape(1, d_model)
    return pl.pallas_call(
        _out_proj_kernel,
        grid=(bn, s // tm),
        in_specs=[row(d_model),
                  pl.BlockSpec((None, 1, 3 * d_model), lambda b, i: (b, 0, 0)),
                  whole(g2), whole(w_out_l), whole(spread),
                  row(ya.shape[-1]),
                  row(ob[0].shape[-1]), row(ob[1].shape[-1]), row(ob[2].shape[-1]),
                  row(LANES), row(LANES), row(LANES),
                  row(yc.shape[-1]), row(z.shape[-1])],
        out_specs=row(d_model),
        out_shape=jax.ShapeDtypeStruct(x.shape, x.dtype),
        compiler_params=pltpu.CompilerParams(
            dimension_semantics=("parallel", "parallel"), vmem_limit_bytes=VMEM_LIMIT),
        name="out_proj",
    )(x, mod_l.reshape(bn, 1, 3 * d_model), g2, w_out_l, spread, ya, *ob, *lb, yc, z)


def _prepare_w_in(w, sizes, n_kv_a):
    offs = np.concatenate([[0], np.cumsum(sizes)])
    grp = lambda i: w[:, int(offs[i]):int(offs[i + 1])]
    aq, ak, av, bq, bk, bv, cq, ck, cv, z = [grp(i) for i in range(10)]

    def dup(m):
        parts = []
        for g in range(n_kv_a):
            head = m[:, g * HEAD_DIM:(g + 1) * HEAD_DIM]
            parts += [head, head]
        return jnp.concatenate(parts, axis=1)

    groups = [aq * ATT_SCALE, dup(ak), dup(av), bq * ATT_SCALE, bk, bv, cq * ATT_SCALE, cv, z]
    widths = tuple(int(g.shape[1]) for g in groups)
    w_cols = jnp.concatenate(groups, axis=1).astype(BF16)
    return w_cols, ck.T.astype(BF16), widths


def kernel(x, c, rel_table, w_in, w_out, w_ada, b_ada, g_pre, g_post, a_sinks,
           lam_q1, lam_k1, lam_q2, lam_k2, g_sub):
    bn, s, d_model = x.shape
    depth = w_in.shape[0]
    d_mix = w_out.shape[1]
    a_width = b_width = 3 * d_mix // 8
    c_width = d_mix // 4
    n_a_heads = a_width // HEAD_DIM
    n_kv_a = n_a_heads // 4
    n_b_heads = b_width // HEAD_DIM
    n_c_heads = c_width // (2 * HEAD_DIM)
    sizes = (a_width, n_kv_a * HEAD_DIM, n_kv_a * HEAD_DIM, b_width, b_width, b_width,
             c_width, c_width, c_width, d_mix)
    assert w_in.shape[2] == sum(sizes) and rel_table.shape == (REL_BUCKETS,
                                                                n_a_heads + n_b_heads + n_c_heads)
    assert n_a_heads == n_b_heads and s % (B_PATTERNS[-1][1] * BLK) == 0
    assert s % TILE_M == 0 and TILE_M % BLK == 0

    n_near = -(-(FAR_DIST - 1) // TILE_M) + 1
    bias_a, bias_b, bias_c = _bias_tables(rel_table, n_a_heads, n_b_heads, n_c_heads,
                                          TILE_M, n_near)
    far_bias = rel_table[REL_BUCKETS - 1, n_a_heads + n_b_heads:]
    mod = _modulation(c, w_ada, b_ada)

    head_of_lane = np.arange(b_width) // HEAD_DIM
    spread = jnp.asarray(np.arange(LANES)[:, None] == head_of_lane[None, :], BF16)

    for l in range(depth):
        w_cols, w_kt, widths = _prepare_w_in(w_in[l], sizes, n_kv_a)
        aq, ak, av, bq, bk, bv, cq, cv, z, ckt = _in_proj(x, mod[l], g_pre[l], w_cols, w_kt, widths)

        ya = _banded_attention(aq, ak, av, bias_a, 1, 2, sinks=a_sinks[l])
        ob, lb = [], []
        for p, (_, d) in enumerate(B_PATTERNS):
            o_p, lse_p = _banded_attention(bq, bk, bv, bias_b[p], d, 1)
            ob.append(o_p)
            lb.append(lse_p)

        lam_init = 0.8 - 0.6 * math.exp(-0.3 * l)
        lam_params = jnp.stack([lam_q1[l], lam_k1[l], lam_q2[l], lam_k2[l]]).astype(F32)
        yc = _diff_attention(cq, ckt, cv, bias_c, far_bias, lam_params, g_sub[l], lam_init)

        x = _out_proj(x, mod[l], g_post[l], w_out[l].astype(BF16), spread, ya, ob, lb, yc, z)
    return x
```

```python
import functools
import math

import numpy as np
import jax
import jax.numpy as jnp
from jax import lax
from jax.experimental import pallas as pl
from jax.experimental.pallas import tpu as pltpu

F32 = jnp.float32
BF16 = jnp.bfloat16

HEAD_DIM = 64
LANES = 128
BLK = 128
A_WINDOW = 128
B_PATTERNS = ((128, 1), (512, 4), (2048, 16))
REL_BUCKETS = 32
REL_MAX_DIST = 2048
EPS = 1e-6
NEG = -1e30
ATT_SCALE = 1.0 / math.sqrt(HEAD_DIM)

TILE_M = 512
BAND_BLOCKS = 4
DIFF_ROWS = 256
VMEM_LIMIT = 56 * 1024 * 1024


def _rel_bucket_np(n):
    n = np.maximum(np.asarray(n, np.int64), 0)
    max_exact = REL_BUCKETS // 2
    nf = np.maximum(n, 1).astype(np.float32)
    large = max_exact + (np.log(nf / np.float32(max_exact))
                         / np.float32(math.log(REL_MAX_DIST / max_exact))
                         * np.float32(REL_BUCKETS - max_exact)).astype(np.int32)
    large = np.minimum(large, REL_BUCKETS - 1)
    return np.where(n < max_exact, n, large)


_BUCKET_OF = _rel_bucket_np(np.arange(0, 1 << 18))
_THR = [0] + [int(np.argmax(_BUCKET_OF >= b)) for b in range(1, REL_BUCKETS)]
FAR_DIST = _THR[REL_BUCKETS - 1]


def _bucket_values(dist, dist_scale, lo, hi, table_ref, col):
    b_lo = int(_BUCKET_OF[max(lo, 0) * dist_scale])
    b_hi = int(_BUCKET_OF[max(hi, 0) * dist_scale])
    val = jnp.full(dist.shape, table_ref[b_lo, col], F32)
    for b in range(b_lo + 1, b_hi + 1):
        thr = -(-_THR[b] // dist_scale)
        val = jnp.where(dist >= thr, table_ref[b, col], val)
    return val


def _band_bias_kernel(table_ref, a_ref, b_ref):
    h = pl.program_id(0)
    n_a_heads = pl.num_programs(0)
    row = lax.broadcasted_iota(jnp.int32, (BLK, 2 * BLK), 0)
    col = lax.broadcasted_iota(jnp.int32, (BLK, 2 * BLK), 1)
    dist = row + BLK - col
    first_ok = col >= BLK

    def tile(max_dist, dist_scale, table_col):
        val = _bucket_values(dist, dist_scale, 0, max_dist, table_ref, table_col)
        val = jnp.where((dist >= 0) & (dist <= max_dist), val, NEG)
        return jnp.where(first_ok, val, NEG), val

    a_ref[0], a_ref[1] = tile(A_WINDOW - 1, 1, h)
    for p, (w, d) in enumerate(B_PATTERNS):
        b_ref[p, 0], b_ref[p, 1] = tile(w // d, d, n_a_heads + h)


def _diff_bias_kernel(table_ref, out_ref, *, tile, n_near, head_offset):
    h = pl.program_id(0)
    row = lax.broadcasted_iota(jnp.int32, (BLK, BLK), 0)
    col = lax.broadcasted_iota(jnp.int32, (BLK, BLK), 1)
    sub = tile // BLK
    pieces = {}
    for delta in range(-(sub - 1), n_near * sub):
        if delta < 0:
            pieces[delta] = jnp.full((BLK, BLK), NEG, F32)
            continue
        dist = row - col + delta * BLK
        val = _bucket_values(dist, 1, delta * BLK - (BLK - 1), delta * BLK + BLK - 1,
                             table_ref, head_offset + h)
        pieces[delta] = jnp.where(dist >= 0, val, NEG) if delta == 0 else val
    for dt in range(n_near):
        for a in range(sub):
            for b in range(sub):
                out_ref[dt, a * BLK:(a + 1) * BLK, b * BLK:(b + 1) * BLK] = pieces[dt * sub + a - b]


def _bias_tables(rel_table, n_a_heads, n_b_heads, n_c_heads, tile, n_near):
    smem = pl.BlockSpec(memory_space=pltpu.SMEM)
    bias_a, bias_b = pl.pallas_call(
        _band_bias_kernel,
        grid=(n_a_heads,),
        in_specs=[smem],
        out_specs=[pl.BlockSpec((2, BLK, 2 * BLK), lambda h: (0, h, 0)),
                   pl.BlockSpec((3, 2, BLK, 2 * BLK), lambda h: (0, 0, h, 0))],
        out_shape=[jax.ShapeDtypeStruct((2, n_a_heads * BLK, 2 * BLK), F32),
                   jax.ShapeDtypeStruct((3, 2, n_b_heads * BLK, 2 * BLK), F32)],
        name="band_bias",
    )(rel_table)
    bias_c = pl.pallas_call(
        functools.partial(_diff_bias_kernel, tile=tile, n_near=n_near,
                          head_offset=n_a_heads + n_b_heads),
        grid=(n_c_heads,),
        in_specs=[smem],
        out_specs=pl.BlockSpec((None, n_near, tile, tile), lambda h: (h, 0, 0, 0)),
        out_shape=jax.ShapeDtypeStruct((n_c_heads, n_near, tile, tile), F32),
        compiler_params=pltpu.CompilerParams(vmem_limit_bytes=VMEM_LIMIT),
        name="diff_bias",
    )(rel_table)
    return bias_a, bias_b, bias_c


def _split_bf16(v):
    hi = v.astype(BF16)
    lo = (v - hi.astype(F32)).astype(BF16)
    return hi, lo


def _mod_kernel(c_ref, w_ref, b_ref, out_ref):
    c = c_ref[...]
    s = c / (1.0 + jnp.exp(-c))
    s_hi, s_lo = _split_bf16(s)
    w_hi, w_lo = _split_bf16(w_ref[...])
    acc = jnp.dot(s_hi, w_hi, preferred_element_type=F32)
    acc += jnp.dot(s_lo, w_hi, preferred_element_type=F32)
    acc += jnp.dot(s_hi, w_lo, preferred_element_type=F32)
    out_ref[...] = acc + b_ref[...]


def _modulation(c, w_ada, b_ada):
    depth, d_model, n_out = w_ada.shape
    rows = 8
    c_pad = jnp.zeros((rows, d_model), F32).at[:c.shape[0]].set(c)
    tn = 768
    out = pl.pallas_call(
        _mod_kernel,
        grid=(depth, n_out // tn),
        in_specs=[pl.BlockSpec((rows, d_model), lambda l, j: (0, 0)),
                  pl.BlockSpec((None, d_model, tn), lambda l, j: (l, 0, j)),
                  pl.BlockSpec((None, 1, tn), lambda l, j: (l, 0, j))],
        out_specs=pl.BlockSpec((None, rows, tn), lambda l, j: (l, 0, j)),
        out_shape=jax.ShapeDtypeStruct((depth, rows, n_out), F32),
        name="adaln_mod",
    )(c_pad, w_ada, b_ada.reshape(depth, 1, n_out))
    return out[:, :c.shape[0]]


def _in_proj_kernel(x_ref, mod_ref, g_ref, w_ref, wkt_ref, *out_refs, widths):
    d_model = x_ref.shape[-1]
    x = x_ref[...]
    ms = jnp.mean(x * x, axis=-1, keepdims=True)
    shift = mod_ref[:, :d_model]
    scale = mod_ref[:, d_model:2 * d_model]
    h = (x * lax.rsqrt(ms + EPS)) * g_ref[...]
    h = (h * (1.0 + scale) + shift).astype(BF16)
    *col_refs, kt_ref = out_refs
    off = 0
    for ref, width in zip(col_refs, widths):
        ref[...] = jnp.dot(h, w_ref[:, off:off + width],
                           preferred_element_type=F32).astype(ref.dtype)
        off += width
    kt_ref[...] = lax.dot_general(wkt_ref[...], h, (((1,), (1,)), ((), ())),
                                  preferred_element_type=F32).astype(kt_ref.dtype)


def _in_proj(x, mod_l, g_pre_l, w_cols, w_kt, widths):
    bn, s, d_model = x.shape
    tm = TILE_M
    n_t = s // tm
    n_cols = w_cols.shape[1]
    kt_rows = w_kt.shape[0]
    out_shape = [jax.ShapeDtypeStruct((bn, s, w), BF16) for w in widths]
    out_shape.append(jax.ShapeDtypeStruct((bn, n_t, kt_rows, tm), BF16))
    out_specs = [pl.BlockSpec((None, tm, w), lambda b, i: (b, i, 0)) for w in widths]
    out_specs.append(pl.BlockSpec((None, None, kt_rows, tm), lambda b, i: (b, i, 0, 0)))
    return pl.pallas_call(
        functools.partial(_in_proj_kernel, widths=widths),
        grid=(bn, n_t),
        in_specs=[pl.BlockSpec((None, tm, d_model), lambda b, i: (b, i, 0)),
                  pl.BlockSpec((None, 1, 3 * d_model), lambda b, i: (b, 0, 0)),
                  pl.BlockSpec((1, d_model), lambda b, i: (0, 0)),
                  pl.BlockSpec((d_model, n_cols), lambda b, i: (0, 0),
                               pipeline_mode=pl.Buffered(1)),
                  pl.BlockSpec((kt_rows, d_model), lambda b, i: (0, 0),
                               pipeline_mode=pl.Buffered(1))],
        out_specs=out_specs,
        out_shape=out_shape,
        compiler_params=pltpu.CompilerParams(
            dimension_semantics=("parallel", "parallel"), vmem_limit_bytes=VMEM_LIMIT),
        name="in_proj",
    )(x, mod_l.reshape(bn, 1, 3 * d_model), g_pre_l.reshape(1, d_model), w_cols, w_kt)


def _lane_masks():
    lane = lax.broadcasted_iota(jnp.int32, (1, LANES), 1)
    even = lane < HEAD_DIM
    m_even = jnp.where(even, 1.0, 0.0).astype(BF16)
    m_odd = jnp.where(even, 0.0, 1.0).astype(BF16)
    return even, m_even, m_odd


def _banded_kernel(*refs, n_pairs, pairs_per_kv, has_sink):
    if has_sink:
        sink_ref, q_ref, kp_ref, kc_ref, vp_ref, vc_ref, bias_ref, o_ref, k_buf, v_buf = refs
        lse_ref = None
    else:
        q_ref, kp_ref, kc_ref, vp_ref, vc_ref, bias_ref, o_ref, lse_ref, k_buf, v_buf = refs
    step = pl.program_id(2)
    nblk = q_ref.shape[0] // BLK
    k_buf[:BLK] = kp_ref[...]
    k_buf[BLK:] = kc_ref[...]
    v_buf[:BLK] = vp_ref[...]
    v_buf[BLK:] = vc_ref[...]
    even, m_even, m_odd = _lane_masks()
    lane = lax.broadcasted_iota(jnp.int32, (BLK, LANES), 1)
    top_rows = lax.broadcasted_iota(jnp.int32, (2 * BLK, 1), 0) < BLK

    def block(ib, carry):
        r0 = pl.multiple_of(ib * BLK, BLK)
        variant = jnp.where((step == 0) & (ib == 0), 0, 1)
        lse_tile = jnp.zeros((BLK, LANES), F32)
        for j in range(n_pairs):
            g = j // pairs_per_kv
            q2 = q_ref[pl.ds(r0, BLK), j * LANES:(j + 1) * LANES]
            qs = jnp.concatenate([q2 * m_even, q2 * m_odd], axis=0)
            k2 = k_buf[pl.ds(r0, 2 * BLK), g * LANES:(g + 1) * LANES]
            v2 = v_buf[pl.ds(r0, 2 * BLK), g * LANES:(g + 1) * LANES]
            s = lax.dot_general(qs, k2, (((1,), (1,)), ((), ())), preferred_element_type=F32)
            s = s + bias_ref[variant, j * 2 * BLK:(j + 1) * 2 * BLK, :]
            m = jnp.max(s, axis=-1, keepdims=True)
            if has_sink:
                sk = jnp.where(top_rows, sink_ref[2 * j], sink_ref[2 * j + 1])
                m = jnp.maximum(m, sk)
            e = jnp.exp(s - m)
            den = jnp.sum(e, axis=-1, keepdims=True)
            if has_sink:
                den = den + jnp.exp(sk - m)
            o2 = jnp.dot(e.astype(BF16), v2, preferred_element_type=F32) / den
            o_ref[pl.ds(r0, BLK), j * LANES:(j + 1) * LANES] = (
                jnp.where(even, o2[:BLK], o2[BLK:]).astype(o_ref.dtype))
            if lse_ref is not None:
                lse = m + jnp.log(den)
                lse_tile = jnp.where(lane == 2 * j, lse[:BLK], lse_tile)
                lse_tile = jnp.where(lane == 2 * j + 1, lse[BLK:], lse_tile)
        if lse_ref is not None:
            lse_ref[pl.ds(r0, BLK), :] = lse_tile
        return carry

    lax.fori_loop(0, nblk, block, 0)


def _banded_attention(q, k, v, bias, dilation, pairs_per_kv, sinks=None):
    bn, s, wq = q.shape
    wkv = k.shape[-1]
    d = dilation
    sd = s // d
    nblk = min(BAND_BLOCKS, sd // BLK)
    rows = nblk * BLK
    n_steps = sd // rows
    n_pairs = wq // LANES
    qv = q.reshape(bn, sd, d * wq)
    kv_ = k.reshape(bn, sd, d * wkv)
    vv = v.reshape(bn, sd, d * wkv)
    cur = lambda b, r, n: (b, n, r)
    prev = lambda b, r, n: (b, jnp.maximum(n * nblk - 1, 0), r)
    in_specs = [pl.BlockSpec((None, rows, wq), cur),
                pl.BlockSpec((None, BLK, wkv), prev),
                pl.BlockSpec((None, rows, wkv), cur),
                pl.BlockSpec((None, BLK, wkv), prev),
                pl.BlockSpec((None, rows, wkv), cur),
                pl.BlockSpec(bias.shape, lambda b, r, n: (0, 0, 0))]
    args = [qv, kv_, kv_, vv, vv, bias]
    out_specs = [pl.BlockSpec((None, rows, wq), cur)]
    out_shape = [jax.ShapeDtypeStruct((bn, sd, d * wq), BF16)]
    has_sink = sinks is not None
    if has_sink:
        in_specs.insert(0, pl.BlockSpec(memory_space=pltpu.SMEM))
        args.insert(0, sinks)
    else:
        out_specs.append(pl.BlockSpec((None, rows, LANES), cur))
        out_shape.append(jax.ShapeDtypeStruct((bn, sd, d * LANES), F32))
    outs = pl.pallas_call(
        functools.partial(_banded_kernel, n_pairs=n_pairs, pairs_per_kv=pairs_per_kv,
                          has_sink=has_sink),
        grid=(bn, d, n_steps),
        in_specs=in_specs,
        out_specs=out_specs,
        out_shape=out_shape,
        scratch_shapes=[pltpu.VMEM((rows + BLK, wkv), BF16), pltpu.VMEM((rows + BLK, wkv), BF16)],
        compiler_params=pltpu.CompilerParams(
            dimension_semantics=("parallel", "parallel", "arbitrary"),
            vmem_limit_bytes=VMEM_LIMIT),
        name="banded_sink" if has_sink else f"banded_d{d}",
    )(*args)
    o = outs[0].reshape(bn, s, wq)
    if has_sink:
        return o
    return o, outs[1].reshape(bn, s, LANES)


def _diff_kernel(far_ref, q_ref, kt_ref, v_ref, bias_ref, lam_ref, g_ref, o_ref,
                 qs_s, vx_s, m_s, acc_s, *, lam_init, n_near):
    h = pl.program_id(1)
    i = pl.program_id(2)
    t = q_ref.shape[0]
    n_rep = t // LANES

    @pl.when(i == 0)
    def _():
        vx_s[:, :LANES] = v_ref[...]
        vx_s[:, LANES:] = jnp.ones((vx_s.shape[0], LANES), BF16)

    _, m_even, m_odd = _lane_masks()
    q2 = q_ref[...]
    qs_s[:t] = q2 * m_even
    qs_s[t:] = q2 * m_odd
    m_s[...] = jnp.full(m_s.shape, NEG, F32)
    acc_s[...] = jnp.zeros(acc_s.shape, F32)
    far_bias = far_ref[h]

    def tile(j, near):
        k0 = pl.multiple_of(j * t, t)
        kt = kt_ref[j]
        vx = vx_s[pl.ds(k0, t), :]
        for r0 in range(0, 2 * t, DIFF_ROWS):
            rows = pl.ds(r0, DIFF_ROWS)
            s = jnp.dot(qs_s[rows, :], kt, preferred_element_type=F32)
            if near:
                s = s + bias_ref[i - j, pl.ds(r0 % t, DIFF_ROWS), :]
                m_cur = jnp.max(s, axis=-1, keepdims=True)
                shift = 0.0
            else:
                m_cur = jnp.max(s, axis=-1, keepdims=True) + far_bias
                shift = far_bias
            m_old = m_s[rows, :]
            m_new = jnp.maximum(m_old, m_cur)
            alpha = jnp.exp(m_old - m_new)
            m_sub = m_new - shift
            p = jnp.exp(s - jnp.concatenate([m_sub] * n_rep, axis=1))
            pv = jnp.dot(p.astype(BF16), vx, preferred_element_type=F32)
            acc_s[rows, :] = jnp.concatenate([alpha, alpha], axis=1) * acc_s[rows, :] + pv
            m_s[rows, :] = m_new

    def far_tile(j, carry):
        tile(j, False)
        return carry

    def near_tile(j, carry):
        tile(j, True)
        return carry

    n_far = jnp.maximum(i - (n_near - 1), 0)
    lax.fori_loop(0, n_far, far_tile, 0)
    lax.fori_loop(n_far, i + 1, near_tile, 0)

    acc = acc_s[...]
    o = acc[:, :LANES] / acc[:, LANES:]
    lam = (jnp.exp(jnp.sum(lam_ref[0:1, :] * lam_ref[1:2, :], axis=-1, keepdims=True))
           - jnp.exp(jnp.sum(lam_ref[2:3, :] * lam_ref[3:4, :], axis=-1, keepdims=True)) + lam_init)
    y = o[:t] - lam * o[t:]
    y = y * lax.rsqrt(jnp.mean(y * y, axis=-1, keepdims=True) + EPS)
    o_ref[...] = (y * g_ref[...] * (1.0 - lam_init)).astype(o_ref.dtype)


def _diff_attention(q, kt, v, bias_c, far_bias, lam_params, g_sub_l, lam_init):
    bn, s, w = q.shape
    t = TILE_M
    n_t = s // t
    n_heads = w // LANES
    n_near = bias_c.shape[1]
    return pl.pallas_call(
        functools.partial(_diff_kernel, lam_init=lam_init, n_near=n_near),
        grid=(bn, n_heads, n_t),
        in_specs=[pl.BlockSpec(memory_space=pltpu.SMEM),
                  pl.BlockSpec((None, t, LANES), lambda b, h, i: (b, i, h)),
                  pl.BlockSpec((None, n_t, LANES, t), lambda b, h, i: (b, 0, h, 0)),
                  pl.BlockSpec((None, s, LANES), lambda b, h, i: (b, 0, h)),
                  pl.BlockSpec((None, n_near, t, t), lambda b, h, i: (h, 0, 0, 0)),
                  pl.BlockSpec((4, HEAD_DIM), lambda b, h, i: (0, 0)),
                  pl.BlockSpec((1, LANES), lambda b, h, i: (0, 0))],
        out_specs=pl.BlockSpec((None, t, LANES), lambda b, h, i: (b, i, h)),
        out_shape=jax.ShapeDtypeStruct((bn, s, w), BF16),
        scratch_shapes=[pltpu.VMEM((2 * t, LANES), BF16),
                        pltpu.VMEM((s, 2 * LANES), BF16),
                        pltpu.VMEM((2 * t, LANES), F32),
                        pltpu.VMEM((2 * t, 2 * LANES), F32)],
        compiler_params=pltpu.CompilerParams(
            dimension_semantics=("parallel", "parallel", "arbitrary"),
            vmem_limit_bytes=VMEM_LIMIT),
        name="diff_attn",
    )(far_bias, q, kt, v, bias_c, lam_params, g_sub_l.reshape(1, LANES))


def _out_proj_kernel(x_ref, mod_ref, g_ref, w_ref, e_ref, ya_ref, o1_ref, o2_ref, o3_ref,
                     l1_ref, l2_ref, l3_ref, yc_ref, z_ref, out_ref):
    d_model = x_ref.shape[-1]
    wa = ya_ref.shape[-1]
    wb = o1_ref.shape[-1]
    z = z_ref[...].astype(F32)
    sz = z / (1.0 + jnp.exp(-z))

    lses = [l1_ref[...], l2_ref[...], l3_ref[...]]
    mx = jnp.maximum(jnp.maximum(lses[0], lses[1]), lses[2])
    ws = [jnp.exp(l - mx) for l in lses]
    tot = ws[0] + ws[1] + ws[2]
    yb = jnp.zeros((x_ref.shape[0], wb), F32)
    for wgt, o_ref in zip(ws, (o1_ref, o2_ref, o3_ref)):
        a_hi, a_lo = _split_bf16(wgt / tot)
        spread = (jnp.dot(a_hi, e_ref[...], preferred_element_type=F32)
                  + jnp.dot(a_lo, e_ref[...], preferred_element_type=F32))
        yb = yb + spread * o_ref[...].astype(F32)

    ga = (ya_ref[...].astype(F32) * sz[:, :wa]).astype(BF16)
    gb = (yb * sz[:, wa:wa + wb]).astype(BF16)
    gc = (yc_ref[...].astype(F32) * sz[:, wa + wb:]).astype(BF16)
    y = jnp.dot(ga, w_ref[:wa], preferred_element_type=F32)
    y += jnp.dot(gb, w_ref[wa:wa + wb], preferred_element_type=F32)
    y += jnp.dot(gc, w_ref[wa + wb:], preferred_element_type=F32)
    r = (y * lax.rsqrt(jnp.mean(y * y, axis=-1, keepdims=True) + EPS)) * g_ref[...]
    gate = mod_ref[:, 2 * d_model:]
    out_ref[...] = x_ref[...] + gate * r


def _out_proj(x, mod_l, g_post_l, w_out_l, spread, ya, ob, lb, yc, z):
    bn, s, d_model = x.shape
    tm = TILE_M
    row = lambda w: pl.BlockSpec((None, tm, w), lambda b, i: (b, i, 0))
    whole = lambda a: pl.BlockSpec(a.shape, lambda b, i: (0,) * a.ndim,
                                   pipeline_mode=pl.Buffered(1))
    g2 = g_post_l.reshape(1, d_model)
    return pl.pallas_call(
        _out_proj_kernel,
        grid=(bn, s // tm),
        in_specs=[row(d_model),
                  pl.BlockSpec((None, 1, 3 * d_model), lambda b, i: (b, 0, 0)),
                  whole(g2), whole(w_out_l), whole(spread),
                  row(ya.shape[-1]),
                  row(ob[0].shape[-1]), row(ob[1].shape[-1]), row(ob[2].shape[-1]),
                  row(LANES), row(LANES), row(LANES),
                  row(yc.shape[-1]), row(z.shape[-1])],
        out_specs=row(d_model),
        out_shape=jax.ShapeDtypeStruct(x.shape, x.dtype),
        compiler_params=pltpu.CompilerParams(
            dimension_semantics=("parallel", "parallel"), vmem_limit_bytes=VMEM_LIMIT),
        name="out_proj",
    )(x, mod_l.reshape(bn, 1, 3 * d_model), g2, w_out_l, spread, ya, *ob, *lb, yc, z)


def _prepare_w_in(w, sizes, n_kv_a):
    offs = np.concatenate([[0], np.cumsum(sizes)])
    grp = lambda i: w[:, int(offs[i]):int(offs[i + 1])]
    aq, ak, av, bq, bk, bv, cq, ck, cv, z = [grp(i) for i in range(10)]

    def dup(m):
        parts = []
        for g in range(n_kv_a):
            head = m[:, g * HEAD_DIM:(g + 1) * HEAD_DIM]
            parts += [head, head]
        return jnp.concatenate(parts, axis=1)

    groups = [aq * ATT_SCALE, dup(ak), dup(av), bq * ATT_SCALE, bk, bv, cq * ATT_SCALE, cv, z]
    widths = tuple(int(g.shape[1]) for g in groups)
    w_cols = jnp.concatenate(groups, axis=1).astype(BF16)
    return w_cols, ck.T.astype(BF16), widths


def kernel(x, c, rel_table, w_in, w_out, w_ada, b_ada, g_pre, g_post, a_sinks,
           lam_q1, lam_k1, lam_q2, lam_k2, g_sub):
    bn, s, d_model = x.shape
    depth = w_in.shape[0]
    d_mix = w_out.shape[1]
    a_width = b_width = 3 * d_mix // 8
    c_width = d_mix // 4
    n_a_heads = a_width // HEAD_DIM
    n_kv_a = n_a_heads // 4
    n_b_heads = b_width // HEAD_DIM
    n_c_heads = c_width // (2 * HEAD_DIM)
    sizes = (a_width, n_kv_a * HEAD_DIM, n_kv_a * HEAD_DIM, b_width, b_width, b_width,
             c_width, c_width, c_width, d_mix)
    assert w_in.shape[2] == sum(sizes) and rel_table.shape == (REL_BUCKETS,
                                                                n_a_heads + n_b_heads + n_c_heads)
    assert n_a_heads == n_b_heads and s % (B_PATTERNS[-1][1] * BLK) == 0
    assert s % TILE_M == 0 and TILE_M % BLK == 0

    n_near = -(-(FAR_DIST - 1) // TILE_M) + 1
    bias_a, bias_b, bias_c = _bias_tables(rel_table, n_a_heads, n_b_heads, n_c_heads,
                                          TILE_M, n_near)
    far_bias = rel_table[REL_BUCKETS - 1, n_a_heads + n_b_heads:]
    mod = _modulation(c, w_ada, b_ada)

    head_of_lane = np.arange(b_width) // HEAD_DIM
    spread = jnp.asarray(np.arange(LANES)[:, None] == head_of_lane[None, :], BF16)

    for l in range(depth):
        w_cols, w_kt, widths = _prepare_w_in(w_in[l], sizes, n_kv_a)
        aq, ak, av, bq, bk, bv, cq, cv, z, ckt = _in_proj(x, mod[l], g_pre[l], w_cols, w_kt, widths)

        ya = _banded_attention(aq, ak, av, bias_a, 1, 2, sinks=a_sinks[l])
        ob, lb = [], []
        for p, (_, d) in enumerate(B_PATTERNS):
            o_p, lse_p = _banded_attention(bq, bk, bv, bias_b[p], d, 1)
            ob.append(o_p)
            lb.append(lse_p)

        lam_init = 0.8 - 0.6 * math.exp(-0.3 * l)
        lam_params = jnp.stack([lam_q1[l], lam_k1[l], lam_q2[l], lam_k2[l]]).astype(F32)
        yc = _diff_attention(cq, ckt, cv, bias_c, far_bias, lam_params, g_sub[l], lam_init)

        x = _out_proj(x, mod[l], g_post[l], w_out[l].astype(BF16), spread, ya, ob, lb, yc, z)
    return x
```

```python
import functools
import math

import numpy as np
import jax
import jax.numpy as jnp
from jax import lax
from jax.experimental import pallas as pl
from jax.experimental.pallas import tpu as pltpu

F32 = jnp.float32
BF16 = jnp.bfloat16

HEAD_DIM = 64
LANES = 128
BLK = 128
A_WINDOW = 128
B_PATTERNS = ((128, 1), (512, 4), (2048, 16))
REL_BUCKETS = 32
REL_MAX_DIST = 2048
EPS = 1e-6
NEG = -1e30
LOG2E = 1.4426950408889634
Q_SCALE = LOG2E / math.sqrt(HEAD_DIM)

TILE_M = 512
BAND_BLOCKS = 4
DIFF_ROWS = 128
VMEM_LIMIT = 56 * 1024 * 1024


def _rel_bucket_np(n):
    n = np.maximum(np.asarray(n, np.int64), 0)
    max_exact = REL_BUCKETS // 2
    nf = np.maximum(n, 1).astype(np.float32)
    large = max_exact + (np.log(nf / np.float32(max_exact))
                         / np.float32(math.log(REL_MAX_DIST / max_exact))
                         * np.float32(REL_BUCKETS - max_exact)).astype(np.int32)
    large = np.minimum(large, REL_BUCKETS - 1)
    return np.where(n < max_exact, n, large)


_BUCKET_OF = _rel_bucket_np(np.arange(0, 1 << 18))
_THR = [0] + [int(np.argmax(_BUCKET_OF >= b)) for b in range(1, REL_BUCKETS)]
FAR_DIST = _THR[REL_BUCKETS - 1]


def _bucket_values(dist, dist_scale, lo, hi, table_ref, col):
    b_lo = int(_BUCKET_OF[max(lo, 0) * dist_scale])
    b_hi = int(_BUCKET_OF[max(hi, 0) * dist_scale])
    val = jnp.full(dist.shape, table_ref[b_lo, col] * LOG2E, F32)
    for b in range(b_lo + 1, b_hi + 1):
        thr = -(-_THR[b] // dist_scale)
        val = jnp.where(dist >= thr, table_ref[b, col] * LOG2E, val)
    return val


def _band_bias_kernel(table_ref, a_ref, b_ref):
    h = pl.program_id(0)
    n_a_heads = pl.num_programs(0)
    row = lax.broadcasted_iota(jnp.int32, (BLK, 2 * BLK), 0)
    col = lax.broadcasted_iota(jnp.int32, (BLK, 2 * BLK), 1)
    dist = row + BLK - col
    first_ok = col >= BLK

    def tile(max_dist, dist_scale, table_col):
        val = _bucket_values(dist, dist_scale, 0, max_dist, table_ref, table_col)
        val = jnp.where((dist >= 0) & (dist <= max_dist), val, NEG)
        return jnp.where(first_ok, val, NEG), val

    a_ref[0], a_ref[1] = tile(A_WINDOW - 1, 1, h)
    for p, (w, d) in enumerate(B_PATTERNS):
        b_ref[p, 0], b_ref[p, 1] = tile(w // d, d, n_a_heads + h)


def _diff_bias_kernel(table_ref, out_ref, *, tile, n_tiles, head_offset):
    h = pl.program_id(0)
    row = lax.broadcasted_iota(jnp.int32, (BLK, BLK), 0)
    col = lax.broadcasted_iota(jnp.int32, (BLK, BLK), 1)
    sub = tile // BLK
    pieces = {}
    for delta in range(-(sub - 1), n_tiles * sub):
        if delta < 0:
            pieces[delta] = jnp.full((BLK, BLK), NEG, F32)
            continue
        dist = row - col + delta * BLK
        val = _bucket_values(dist, 1, delta * BLK - (BLK - 1), delta * BLK + BLK - 1,
                             table_ref, head_offset + h)
        pieces[delta] = jnp.where(dist >= 0, val, NEG) if delta == 0 else val
    for dt in range(n_tiles):
        for a in range(sub):
            for b in range(sub):
                out_ref[dt, a * BLK:(a + 1) * BLK, b * BLK:(b + 1) * BLK] = pieces[dt * sub + a - b]


def _bias_tables(rel_table, n_a_heads, n_b_heads, n_c_heads, tile, n_tiles):
    smem = pl.BlockSpec(memory_space=pltpu.SMEM)
    bias_a, bias_b = pl.pallas_call(
        _band_bias_kernel,
        grid=(n_a_heads,),
        in_specs=[smem],
        out_specs=[pl.BlockSpec((2, BLK, 2 * BLK), lambda h: (0, h, 0)),
                   pl.BlockSpec((3, 2, BLK, 2 * BLK), lambda h: (0, 0, h, 0))],
        out_shape=[jax.ShapeDtypeStruct((2, n_a_heads * BLK, 2 * BLK), F32),
                   jax.ShapeDtypeStruct((3, 2, n_b_heads * BLK, 2 * BLK), F32)],
        name="band_bias",
    )(rel_table)
    bias_c = pl.pallas_call(
        functools.partial(_diff_bias_kernel, tile=tile, n_tiles=n_tiles,
                          head_offset=n_a_heads + n_b_heads),
        grid=(n_c_heads,),
        in_specs=[smem],
        out_specs=pl.BlockSpec((None, n_tiles, tile, tile), lambda h: (h, 0, 0, 0)),
        out_shape=jax.ShapeDtypeStruct((n_c_heads, n_tiles, tile, tile), F32),
        compiler_params=pltpu.CompilerParams(vmem_limit_bytes=VMEM_LIMIT),
        name="diff_bias",
    )(rel_table)
    return bias_a, bias_b, bias_c


def _split_bf16(v):
    hi = v.astype(BF16)
    lo = (v - hi.astype(F32)).astype(BF16)
    return hi, lo


def _mod_kernel(c_ref, w_ref, b_ref, out_ref):
    c = c_ref[...]
    s = c / (1.0 + jnp.exp(-c))
    s_hi, s_lo = _split_bf16(s)
    w_hi, w_lo = _split_bf16(w_ref[...])
    acc = jnp.dot(s_hi, w_hi, preferred_element_type=F32)
    acc += jnp.dot(s_lo, w_hi, preferred_element_type=F32)
    acc += jnp.dot(s_hi, w_lo, preferred_element_type=F32)
    out_ref[...] = acc + b_ref[...]


def _modulation(c, w_ada, b_ada):
    depth, d_model, n_out = w_ada.shape
    rows = 8
    c_pad = jnp.zeros((rows, d_model), F32).at[:c.shape[0]].set(c)
    tn = 768
    out = pl.pallas_call(
        _mod_kernel,
        grid=(depth, n_out // tn),
        in_specs=[pl.BlockSpec((rows, d_model), lambda l, j: (0, 0)),
                  pl.BlockSpec((None, d_model, tn), lambda l, j: (l, 0, j)),
                  pl.BlockSpec((None, 1, tn), lambda l, j: (l, 0, j))],
        out_specs=pl.BlockSpec((None, rows, tn), lambda l, j: (l, 0, j)),
        out_shape=jax.ShapeDtypeStruct((depth, rows, n_out), F32),
        name="adaln_mod",
    )(c_pad, w_ada, b_ada.reshape(depth, 1, n_out))
    return out[:, :c.shape[0]]


def _in_proj_kernel(x_ref, mod_ref, g_ref, w_ref, wkt_ref, *out_refs, widths):
    d_model = x_ref.shape[-1]
    x = x_ref[...]
    ms = jnp.mean(x * x, axis=-1, keepdims=True)
    shift = mod_ref[:, :d_model]
    scale = mod_ref[:, d_model:2 * d_model]
    h = (x * lax.rsqrt(ms + EPS)) * g_ref[...]
    h = (h * (1.0 + scale) + shift).astype(BF16)
    *col_refs, kt_ref = out_refs
    off = 0
    for ref, width in zip(col_refs, widths):
        ref[...] = jnp.dot(h, w_ref[:, off:off + width],
                           preferred_element_type=F32).astype(ref.dtype)
        off += width
    kt_ref[...] = lax.dot_general(wkt_ref[...], h, (((1,), (1,)), ((), ())),
                                  preferred_element_type=F32).astype(kt_ref.dtype)


def _in_proj(x, mod_l, g_pre_l, w_cols, w_kt, widths):
    bn, s, d_model = x.shape
    tm = TILE_M
    n_t = s // tm
    n_cols = w_cols.shape[1]
    kt_rows = w_kt.shape[0]
    out_shape = [jax.ShapeDtypeStruct((bn, s, w), BF16) for w in widths]
    out_shape.append(jax.ShapeDtypeStruct((bn, n_t, kt_rows, tm), BF16))
    out_specs = [pl.BlockSpec((None, tm, w), lambda b, i: (b, i, 0)) for w in widths]
    out_specs.append(pl.BlockSpec((None, None, kt_rows, tm), lambda b, i: (b, i, 0, 0)))
    return pl.pallas_call(
        functools.partial(_in_proj_kernel, widths=widths),
        grid=(bn, n_t),
        in_specs=[pl.BlockSpec((None, tm, d_model), lambda b, i: (b, i, 0)),
                  pl.BlockSpec((None, 1, 3 * d_model), lambda b, i: (b, 0, 0)),
                  pl.BlockSpec((1, d_model), lambda b, i: (0, 0)),
                  pl.BlockSpec((d_model, n_cols), lambda b, i: (0, 0),
                               pipeline_mode=pl.Buffered(1)),
                  pl.BlockSpec((kt_rows, d_model), lambda b, i: (0, 0),
                               pipeline_mode=pl.Buffered(1))],
        out_specs=out_specs,
        out_shape=out_shape,
        compiler_params=pltpu.CompilerParams(
            dimension_semantics=("parallel", "parallel"), vmem_limit_bytes=VMEM_LIMIT),
        name="in_proj",
    )(x, mod_l.reshape(bn, 1, 3 * d_model), g_pre_l.reshape(1, d_model), w_cols, w_kt)


def _lane_masks():
    lane = lax.broadcasted_iota(jnp.int32, (1, LANES), 1)
    even = lane < HEAD_DIM
    m_even = jnp.where(even, 1.0, 0.0).astype(BF16)
    m_odd = jnp.where(even, 0.0, 1.0).astype(BF16)
    return even, m_even, m_odd


def _banded_kernel(*refs, n_pairs, pairs_per_kv, has_sink):
    if has_sink:
        sink_ref, q_ref, kp_ref, kc_ref, vp_ref, vc_ref, bias_ref, o_ref, k_buf, v_buf = refs
        lse_ref = None
    else:
        q_ref, kp_ref, kc_ref, vp_ref, vc_ref, bias_ref, o_ref, lse_ref, k_buf, v_buf = refs
    step = pl.program_id(2)
    nblk = q_ref.shape[0] // BLK
    k_buf[:BLK] = kp_ref[...]
    k_buf[BLK:] = kc_ref[...]
    v_buf[:BLK] = vp_ref[...]
    v_buf[BLK:] = vc_ref[...]
    even, m_even, m_odd = _lane_masks()
    lane = lax.broadcasted_iota(jnp.int32, (BLK, LANES), 1)
    top_rows = lax.broadcasted_iota(jnp.int32, (2 * BLK, 1), 0) < BLK
    ones = jnp.ones((2 * BLK, LANES), BF16)

    def block(ib, carry):
        r0 = pl.multiple_of(ib * BLK, BLK)
        variant = jnp.where((step == 0) & (ib == 0), 0, 1)
        lse_tile = jnp.zeros((BLK, LANES), F32)
        for j in range(n_pairs):
            g = j // pairs_per_kv
            q2 = q_ref[pl.ds(r0, BLK), j * LANES:(j + 1) * LANES]
            qs = jnp.concatenate([q2 * m_even, q2 * m_odd], axis=0)
            k2 = k_buf[pl.ds(r0, 2 * BLK), g * LANES:(g + 1) * LANES]
            v2 = v_buf[pl.ds(r0, 2 * BLK), g * LANES:(g + 1) * LANES]
            s = lax.dot_general(qs, k2, (((1,), (1,)), ((), ())), preferred_element_type=F32)
            s = s + bias_ref[variant, j * 2 * BLK:(j + 1) * 2 * BLK, :]
            m = jnp.max(s, axis=-1, keepdims=True)
            if has_sink:
                sk = jnp.where(top_rows, sink_ref[2 * j], sink_ref[2 * j + 1]) * LOG2E
                m = jnp.maximum(m, sk)
            e = jnp.exp2(s - m)
            if pairs_per_kv > 1:
                ox = jnp.dot(e.astype(BF16), jnp.concatenate([v2, ones], axis=1),
                             preferred_element_type=F32)
                ov, den = ox[:, :LANES], ox[:, LANES:]
            else:
                den = jnp.sum(e, axis=-1, keepdims=True)
                ov = jnp.dot(e.astype(BF16), v2, preferred_element_type=F32)
            if has_sink:
                den = den + jnp.exp2(sk - m)
            o2 = ov / den
            o_ref[pl.ds(r0, BLK), j * LANES:(j + 1) * LANES] = (
                jnp.where(even, o2[:BLK], o2[BLK:]).astype(o_ref.dtype))
            if lse_ref is not None:
                lse = m + jnp.log2(den)
                lse_tile = jnp.where(lane == 2 * j, lse[:BLK], lse_tile)
                lse_tile = jnp.where(lane == 2 * j + 1, lse[BLK:], lse_tile)
        if lse_ref is not None:
            lse_ref[pl.ds(r0, BLK), :] = lse_tile
        return carry

    lax.fori_loop(0, nblk, block, 0)


def _banded_attention(q, k, v, bias, dilation, pairs_per_kv, sinks=None):
    bn, s, wq = q.shape
    wkv = k.shape[-1]
    d = dilation
    sd = s // d
    nblk = min(BAND_BLOCKS, sd // BLK)
    rows = nblk * BLK
    n_steps = sd // rows
    n_pairs = wq // LANES
    qv = q.reshape(bn, sd, d * wq)
    kv_ = k.reshape(bn, sd, d * wkv)
    vv = v.reshape(bn, sd, d * wkv)
    cur = lambda b, r, n: (b, n, r)
    prev = lambda b, r, n: (b, jnp.maximum(n * nblk - 1, 0), r)
    in_specs = [pl.BlockSpec((None, rows, wq), cur),
                pl.BlockSpec((None, BLK, wkv), prev),
                pl.BlockSpec((None, rows, wkv), cur),
                pl.BlockSpec((None, BLK, wkv), prev),
                pl.BlockSpec((None, rows, wkv), cur),
                pl.BlockSpec(bias.shape, lambda b, r, n: (0, 0, 0))]
    args = [qv, kv_, kv_, vv, vv, bias]
    out_specs = [pl.BlockSpec((None, rows, wq), cur)]
    out_shape = [jax.ShapeDtypeStruct((bn, sd, d * wq), BF16)]
    has_sink = sinks is not None
    if has_sink:
        in_specs.insert(0, pl.BlockSpec(memory_space=pltpu.SMEM))
        args.insert(0, sinks)
    else:
        out_specs.append(pl.BlockSpec((None, rows, LANES), cur))
        out_shape.append(jax.ShapeDtypeStruct((bn, sd, d * LANES), F32))
    outs = pl.pallas_call(
        functools.partial(_banded_kernel, n_pairs=n_pairs, pairs_per_kv=pairs_per_kv,
                          has_sink=has_sink),
        grid=(bn, d, n_steps),
        in_specs=in_specs,
        out_specs=out_specs,
        out_shape=out_shape,
        scratch_shapes=[pltpu.VMEM((rows + BLK, wkv), BF16), pltpu.VMEM((rows + BLK, wkv), BF16)],
        compiler_params=pltpu.CompilerParams(
            dimension_semantics=("parallel", "parallel", "arbitrary"),
            vmem_limit_bytes=VMEM_LIMIT),
        name="banded_sink" if has_sink else f"banded_d{d}",
    )(*args)
    o = outs[0].reshape(bn, s, wq)
    if has_sink:
        return o
    return o, outs[1].reshape(bn, s, LANES)


def _diff_kernel(far_ref, q_ref, kt_ref, v_ref, bias_ref, lam_ref, g_ref, o_ref,
                 qs_s, vx_s, m_s, acc_s, *, lam_init, n_near):
    h = pl.program_id(1)
    i = pl.program_id(2)
    t = q_ref.shape[0]
    n_rep = t // LANES

    @pl.when(i == 0)
    def _():
        vx_s[:, :LANES] = v_ref[...]
        vx_s[:, LANES:] = jnp.ones((vx_s.shape[0], LANES), BF16)

    _, m_even, m_odd = _lane_masks()
    q2 = q_ref[...]
    qs_s[:t] = q2 * m_even
    qs_s[t:] = q2 * m_odd
    m_s[...] = jnp.full(m_s.shape, NEG, F32)
    acc_s[...] = jnp.zeros(acc_s.shape, F32)
    far_bias = far_ref[h] * LOG2E

    def sweep(j0, n_tiles, near):
        k0 = pl.multiple_of(j0 * t, t)
        kts = [kt_ref[j0 + a] for a in range(n_tiles)]
        vx = vx_s[pl.ds(k0, n_tiles * t), :]
        for r0 in range(0, 2 * t, DIFF_ROWS):
            rows = pl.ds(r0, DIFF_ROWS)
            qc = qs_s[rows, :]
            parts = [jnp.dot(qc, kt, preferred_element_type=F32) for kt in kts]
            if near:
                parts = [part + bias_ref[jnp.minimum(i - j0 - a, n_near),
                                         pl.ds(r0 % t, DIFF_ROWS), :]
                         for a, part in enumerate(parts)]
                shift = 0.0
            else:
                shift = far_bias
            s = parts[0] if n_tiles == 1 else jnp.concatenate(parts, axis=1)
            m_cur = jnp.max(s, axis=-1, keepdims=True) + shift
            m_old = m_s[rows, :]
            m_new = jnp.maximum(m_old, m_cur)
            alpha = jnp.exp2(m_old - m_new)
            m_sub = m_new - shift
            p = jnp.exp2(s - jnp.concatenate([m_sub] * (n_tiles * n_rep), axis=1))
            pv = jnp.dot(p.astype(BF16), vx, preferred_element_type=F32)
            acc_s[rows, :] = jnp.concatenate([alpha, alpha], axis=1) * acc_s[rows, :] + pv
            m_s[rows, :] = m_new

    def far_pair(jj, carry):
        sweep(2 * jj, 2, False)
        return carry

    n_far_pairs = jnp.maximum(i - (n_near - 1), 0) >> 1
    j_near = 2 * n_far_pairs
    odd = (i + 1 - j_near) & 1
    lax.fori_loop(0, n_far_pairs, far_pair, 0)

    @pl.when(odd == 1)
    def _():
        sweep(j_near, 1, True)

    def near_pair(kk, carry):
        sweep(j_near + odd + 2 * kk, 2, True)
        return carry

    lax.fori_loop(0, (i + 1 - j_near) >> 1, near_pair, 0)

    acc = acc_s[...]
    o = acc[:, :LANES] / acc[:, LANES:]
    lam = (jnp.exp(jnp.sum(lam_ref[0:1, :] * lam_ref[1:2, :], axis=-1, keepdims=True))
           - jnp.exp(jnp.sum(lam_ref[2:3, :] * lam_ref[3:4, :], axis=-1, keepdims=True)) + lam_init)
    y = o[:t] - lam * o[t:]
    y = y * lax.rsqrt(jnp.mean(y * y, axis=-1, keepdims=True) + EPS)
    o_ref[...] = (y * g_ref[...] * (1.0 - lam_init)).astype(o_ref.dtype)


def _diff_attention(q, kt, v, bias_c, far_bias, lam_params, g_sub_l, lam_init):
    bn, s, w = q.shape
    t = TILE_M
    n_t = s // t
    n_heads = w // LANES
    n_near = bias_c.shape[1] - 1
    return pl.pallas_call(
        functools.partial(_diff_kernel, lam_init=lam_init, n_near=n_near),
        grid=(bn, n_heads, n_t),
        in_specs=[pl.BlockSpec(memory_space=pltpu.SMEM),
                  pl.BlockSpec((None, t, LANES), lambda b, h, i: (b, i, h)),
                  pl.BlockSpec((None, n_t, LANES, t), lambda b, h, i: (b, 0, h, 0)),
                  pl.BlockSpec((None, s, LANES), lambda b, h, i: (b, 0, h)),
                  pl.BlockSpec((None, n_near + 1, t, t), lambda b, h, i: (h, 0, 0, 0)),
                  pl.BlockSpec((4, HEAD_DIM), lambda b, h, i: (0, 0)),
                  pl.BlockSpec((1, LANES), lambda b, h, i: (0, 0))],
        out_specs=pl.BlockSpec((None, t, LANES), lambda b, h, i: (b, i, h)),
        out_shape=jax.ShapeDtypeStruct((bn, s, w), BF16),
        scratch_shapes=[pltpu.VMEM((2 * t, LANES), BF16),
                        pltpu.VMEM((s, 2 * LANES), BF16),
                        pltpu.VMEM((2 * t, LANES), F32),
                        pltpu.VMEM((2 * t, 2 * LANES), F32)],
        compiler_params=pltpu.CompilerParams(
            dimension_semantics=("parallel", "parallel", "arbitrary"),
            vmem_limit_bytes=VMEM_LIMIT),
        name="diff_attn",
    )(far_bias, q, kt, v, bias_c, lam_params, g_sub_l.reshape(1, LANES))


def _out_proj_kernel(x_ref, mod_ref, g_ref, w_ref, e_ref, ya_ref, o1_ref, o2_ref, o3_ref,
                     l1_ref, l2_ref, l3_ref, yc_ref, z_ref, out_ref):
    d_model = x_ref.shape[-1]
    wa = ya_ref.shape[-1]
    wb = o1_ref.shape[-1]
    z = z_ref[...].astype(F32)
    sz = z / (1.0 + jnp.exp(-z))

    lses = [l1_ref[...], l2_ref[...], l3_ref[...]]
    mx = jnp.maximum(jnp.maximum(lses[0], lses[1]), lses[2])
    ws = [jnp.exp2(l - mx) for l in lses]
    tot = ws[0] + ws[1] + ws[2]
    yb = jnp.zeros((x_ref.shape[0], wb), F32)
    for wgt, o_ref in zip(ws, (o1_ref, o2_ref, o3_ref)):
        a_hi, a_lo = _split_bf16(wgt / tot)
        spread = (jnp.dot(a_hi, e_ref[...], preferred_element_type=F32)
                  + jnp.dot(a_lo, e_ref[...], preferred_element_type=F32))
        yb = yb + spread * o_ref[...].astype(F32)

    ga = (ya_ref[...].astype(F32) * sz[:, :wa]).astype(BF16)
    gb = (yb * sz[:, wa:wa + wb]).astype(BF16)
    gc = (yc_ref[...].astype(F32) * sz[:, wa + wb:]).astype(BF16)
    y = jnp.dot(ga, w_ref[:wa], preferred_element_type=F32)
    y += jnp.dot(gb, w_ref[wa:wa + wb], preferred_element_type=F32)
    y += jnp.dot(gc, w_ref[wa + wb:], preferred_element_type=F32)
    r = (y * lax.rsqrt(jnp.mean(y * y, axis=-1, keepdims=True) + EPS)) * g_ref[...]
    gate = mod_ref[:, 2 * d_model:]
    out_ref[...] = x_ref[...] + gate * r


def _out_proj(x, mod_l, g_post_l, w_out_l, spread, ya, ob, lb, yc, z):
    bn, s, d_model = x.shape
    tm = TILE_M
    row = lambda w: pl.BlockSpec((None, tm, w), lambda b, i: (b, i, 0))
    whole = lambda a: pl.BlockSpec(a.shape, lambda b, i: (0,) * a.ndim,
                                   pipeline_mode=pl.Buffered(1))
    g2 = g_post_l.reshape(1, d_model)
    return pl.pallas_call(
        _out_proj_kernel,
        grid=(bn, s // tm),
        in_specs=[row(d_model),
                  pl.BlockSpec((None, 1, 3 * d_model), lambda b, i: (b, 0, 0)),
                  whole(g2), whole(w_out_l), whole(spread),
                  row(ya.shape[-1]),
                  row(ob[0].shape[-1]), row(ob[1].shape[-1]), row(ob[2].shape[-1]),
                  row(LANES), row(LANES), row(LANES),
                  row(yc.shape[-1]), row(z.shape[-1])],
        out_specs=row(d_model),
        out_shape=jax.ShapeDtypeStruct(x.shape, x.dtype),
        compiler_params=pltpu.CompilerParams(
            dimension_semantics=("parallel", "parallel"), vmem_limit_bytes=VMEM_LIMIT),
        name="out_proj",
    )(x, mod_l.reshape(bn, 1, 3 * d_model), g2, w_out_l, spread, ya, *ob, *lb, yc, z)


def _prepare_w_in(w, sizes, n_kv_a):
    offs = np.concatenate([[0], np.cumsum(sizes)])
    grp = lambda i: w[:, int(offs[i]):int(offs[i + 1])]
    aq, ak, av, bq, bk, bv, cq, ck, cv, z = [grp(i) for i in range(10)]

    def dup(m):
        parts = []
        for g in range(n_kv_a):
            head = m[:, g * HEAD_DIM:(g + 1) * HEAD_DIM]
            parts += [head, head]
        return jnp.concatenate(parts, axis=1)

    groups = [aq * Q_SCALE, dup(ak), dup(av), bq * Q_SCALE, bk, bv, cq * Q_SCALE, cv, z]
    widths = tuple(int(g.shape[1]) for g in groups)
    w_cols = jnp.concatenate(groups, axis=1).astype(BF16)
    return w_cols, ck.T.astype(BF16), widths


def kernel(x, c, rel_table, w_in, w_out, w_ada, b_ada, g_pre, g_post, a_sinks,
           lam_q1, lam_k1, lam_q2, lam_k2, g_sub):
    bn, s, d_model = x.shape
    depth = w_in.shape[0]
    d_mix = w_out.shape[1]
    a_width = b_width = 3 * d_mix // 8
    c_width = d_mix // 4
    n_a_heads = a_width // HEAD_DIM
    n_kv_a = n_a_heads // 4
    n_b_heads = b_width // HEAD_DIM
    n_c_heads = c_width // (2 * HEAD_DIM)
    sizes = (a_width, n_kv_a * HEAD_DIM, n_kv_a * HEAD_DIM, b_width, b_width, b_width,
             c_width, c_width, c_width, d_mix)
    assert w_in.shape[2] == sum(sizes) and rel_table.shape == (REL_BUCKETS,
                                                                n_a_heads + n_b_heads + n_c_heads)
    assert n_a_heads == n_b_heads and s % (B_PATTERNS[-1][1] * BLK) == 0
    assert s % TILE_M == 0 and TILE_M % BLK == 0

    n_near = -(-(FAR_DIST - 1) // TILE_M) + 1
    bias_a, bias_b, bias_c = _bias_tables(rel_table, n_a_heads, n_b_heads, n_c_heads,
                                          TILE_M, n_near + 1)
    far_bias = rel_table[REL_BUCKETS - 1, n_a_heads + n_b_heads:]
    mod = _modulation(c, w_ada, b_ada)

    head_of_lane = np.arange(b_width) // HEAD_DIM
    spread = jnp.asarray(np.arange(LANES)[:, None] == head_of_lane[None, :], BF16)

    for l in range(depth):
        w_cols, w_kt, widths = _prepare_w_in(w_in[l], sizes, n_kv_a)
        aq, ak, av, bq, bk, bv, cq, cv, z, ckt = _in_proj(x, mod[l], g_pre[l], w_cols, w_kt, widths)

        ya = _banded_attention(aq, ak, av, bias_a, 1, 2, sinks=a_sinks[l])
        ob, lb = [], []
        for p, (_, d) in enumerate(B_PATTERNS):
            o_p, lse_p = _banded_attention(bq, bk, bv, bias_b[p], d, 1)
            ob.append(o_p)
            lb.append(lse_p)

        lam_init = 0.8 - 0.6 * math.exp(-0.3 * l)
        lam_params = jnp.stack([lam_q1[l], lam_k1[l], lam_q2[l], lam_k2[l]]).astype(F32)
        yc = _diff_attention(cq, ckt, cv, bias_c, far_bias, lam_params, g_sub[l], lam_init)

        x = _out_proj(x, mod[l], g_post[l], w_out[l].astype(BF16), spread, ya, ob, lb, yc, z)
    return x
```

```python
import functools
import math

import numpy as np
import jax
import jax.numpy as jnp
from jax import lax
from jax.experimental import pallas as pl
from jax.experimental.pallas import tpu as pltpu

F32 = jnp.float32
BF16 = jnp.bfloat16

HEAD_DIM = 64
LANES = 128
BLK = 128
A_WINDOW = 128
B_PATTERNS = ((128, 1), (512, 4), (2048, 16))
REL_BUCKETS = 32
REL_MAX_DIST = 2048
EPS = 1e-6
NEG = -1e30
LOG2E = 1.4426950408889634
Q_SCALE = LOG2E / math.sqrt(HEAD_DIM)

TILE_M = 512
BAND_BLOCKS = 4
N_CLASSES = 16
BAND_PIECES = 4
PIECED_PATTERN = 1
DIFF_ROWS = 128
VMEM_LIMIT = 56 * 1024 * 1024


def _class_residue(c):
    return c // BAND_PIECES + (N_CLASSES // BAND_PIECES) * (c % BAND_PIECES)


def _rel_bucket_np(n):
    n = np.maximum(np.asarray(n, np.int64), 0)
    max_exact = REL_BUCKETS // 2
    nf = np.maximum(n, 1).astype(np.float32)
    large = max_exact + (np.log(nf / np.float32(max_exact))
                         / np.float32(math.log(REL_MAX_DIST / max_exact))
                         * np.float32(REL_BUCKETS - max_exact)).astype(np.int32)
    large = np.minimum(large, REL_BUCKETS - 1)
    return np.where(n < max_exact, n, large)


_BUCKET_OF = _rel_bucket_np(np.arange(0, 1 << 18))
_THR = [0] + [int(np.argmax(_BUCKET_OF >= b)) for b in range(1, REL_BUCKETS)]
FAR_DIST = _THR[REL_BUCKETS - 1]


def _bucket_values(dist, dist_scale, lo, hi, table_ref, col):
    b_lo = int(_BUCKET_OF[max(lo, 0) * dist_scale])
    b_hi = int(_BUCKET_OF[max(hi, 0) * dist_scale])
    val = jnp.full(dist.shape, table_ref[b_lo, col] * LOG2E, F32)
    for b in range(b_lo + 1, b_hi + 1):
        thr = -(-_THR[b] // dist_scale)
        val = jnp.where(dist >= thr, table_ref[b, col] * LOG2E, val)
    return val


def _band_bias_kernel(table_ref, a_ref, b_ref):
    h = pl.program_id(0)
    n_a_heads = pl.num_programs(0)
    row = lax.broadcasted_iota(jnp.int32, (BLK, 2 * BLK), 0)
    col = lax.broadcasted_iota(jnp.int32, (BLK, 2 * BLK), 1)
    first_ok = col >= BLK

    def pieced(idx):
        per = BLK // BAND_PIECES
        return (idx & (per - 1)) * BAND_PIECES + (idx >> (per.bit_length() - 1))

    def tile(dist, max_dist, dist_scale, table_col):
        val = _bucket_values(dist, dist_scale, 0, max_dist, table_ref, table_col)
        val = jnp.where((dist >= 0) & (dist <= max_dist), val, NEG)
        return jnp.where(first_ok, val, NEG), val

    dist = row + BLK - col
    a_ref[0], a_ref[1] = tile(dist, A_WINDOW - 1, 1, h)
    for p, (w, d) in enumerate(B_PATTERNS):
        if p == PIECED_PATTERN:
            within = col & (BLK - 1)
            dist_p = pieced(row) + BLK - (pieced(within) + (col - within))
        else:
            dist_p = dist
        b_ref[p, 0], b_ref[p, 1] = tile(dist_p, w // d, d, n_a_heads + h)


def _diff_bias_kernel(table_ref, out_ref, *, tile, n_tiles, head_offset):
    h = pl.program_id(0)
    row = lax.broadcasted_iota(jnp.int32, (BLK, BLK), 0)
    col = lax.broadcasted_iota(jnp.int32, (BLK, BLK), 1)
    sub = tile // BLK
    pieces = {}
    for delta in range(-(sub - 1), n_tiles * sub):
        if delta < 0:
            pieces[delta] = jnp.full((BLK, BLK), NEG, F32)
            continue
        dist = row - col + delta * BLK
        val = _bucket_values(dist, 1, delta * BLK - (BLK - 1), delta * BLK + BLK - 1,
                             table_ref, head_offset + h)
        pieces[delta] = jnp.where(dist >= 0, val, NEG) if delta == 0 else val
    for dt in range(n_tiles):
        for a in range(sub):
            for b in range(sub):
                out_ref[dt, a * BLK:(a + 1) * BLK, b * BLK:(b + 1) * BLK] = pieces[dt * sub + a - b]


def _bias_tables(rel_table, n_a_heads, n_b_heads, n_c_heads, tile, n_tiles):
    smem = pl.BlockSpec(memory_space=pltpu.SMEM)
    bias_a, bias_b = pl.pallas_call(
        _band_bias_kernel,
        grid=(n_a_heads,),
        in_specs=[smem],
        out_specs=[pl.BlockSpec((2, BLK, 2 * BLK), lambda h: (0, h, 0)),
                   pl.BlockSpec((3, 2, BLK, 2 * BLK), lambda h: (0, 0, h, 0))],
        out_shape=[jax.ShapeDtypeStruct((2, n_a_heads * BLK, 2 * BLK), F32),
                   jax.ShapeDtypeStruct((3, 2, n_b_heads * BLK, 2 * BLK), F32)],
        name="band_bias",
    )(rel_table)
    bias_c = pl.pallas_call(
        functools.partial(_diff_bias_kernel, tile=tile, n_tiles=n_tiles,
                          head_offset=n_a_heads + n_b_heads),
        grid=(n_c_heads,),
        in_specs=[smem],
        out_specs=pl.BlockSpec((None, n_tiles, tile, tile), lambda h: (h, 0, 0, 0)),
        out_shape=jax.ShapeDtypeStruct((n_c_heads, n_tiles, tile, tile), F32),
        compiler_params=pltpu.CompilerParams(vmem_limit_bytes=VMEM_LIMIT),
        name="diff_bias",
    )(rel_table)
    return bias_a, bias_b, bias_c


def _split_bf16(v):
    hi = v.astype(BF16)
    lo = (v - hi.astype(F32)).astype(BF16)
    return hi, lo


def _mod_kernel(c_ref, w_ref, b_ref, out_ref):
    c = c_ref[...]
    s = c / (1.0 + jnp.exp(-c))
    s_hi, s_lo = _split_bf16(s)
    w_hi, w_lo = _split_bf16(w_ref[...])
    acc = jnp.dot(s_hi, w_hi, preferred_element_type=F32)
    acc += jnp.dot(s_lo, w_hi, preferred_element_type=F32)
    acc += jnp.dot(s_hi, w_lo, preferred_element_type=F32)
    out_ref[...] = acc + b_ref[...]


def _modulation(c, w_ada, b_ada):
    depth, d_model, n_out = w_ada.shape
    rows = 8
    c_pad = jnp.zeros((rows, d_model), F32).at[:c.shape[0]].set(c)
    tn = 768
    out = pl.pallas_call(
        _mod_kernel,
        grid=(depth, n_out // tn),
        in_specs=[pl.BlockSpec((rows, d_model), lambda l, j: (0, 0)),
                  pl.BlockSpec((None, d_model, tn), lambda l, j: (l, 0, j)),
                  pl.BlockSpec((None, 1, tn), lambda l, j: (l, 0, j))],
        out_specs=pl.BlockSpec((None, rows, tn), lambda l, j: (l, 0, j)),
        out_shape=jax.ShapeDtypeStruct((depth, rows, n_out), F32),
        name="adaln_mod",
    )(c_pad, w_ada, b_ada.reshape(depth, 1, n_out))
    return out[:, :c.shape[0]]


def _in_proj_kernel(x_ref, mod_ref, g_ref, w_ref, wkt_ref, *refs, widths, cm_groups):
    n_out = len(widths)
    col_refs = refs[:n_out]
    cm_refs = refs[n_out:n_out + len(cm_groups)]
    kt_ref = refs[n_out + len(cm_groups)]
    scr_refs = refs[n_out + len(cm_groups) + 1:]
    d_model = x_ref.shape[-1]
    x = x_ref[...]
    ms = jnp.mean(x * x, axis=-1, keepdims=True)
    shift = mod_ref[:, :d_model]
    scale = mod_ref[:, d_model:2 * d_model]
    h = (x * lax.rsqrt(ms + EPS)) * g_ref[...]
    h = (h * (1.0 + scale) + shift).astype(BF16)
    off = 0
    for gi, (ref, width) in enumerate(zip(col_refs, widths)):
        res = jnp.dot(h, w_ref[:, off:off + width], preferred_element_type=F32)
        ref[...] = res.astype(ref.dtype)
        if gi in cm_groups:
            cm_ref, scr = cm_refs[cm_groups.index(gi)], scr_refs[cm_groups.index(gi)]
            n_cls, per_cls = cm_ref.shape[0], cm_ref.shape[1]
            for k in range(width // LANES):
                lanes = slice(k * LANES, (k + 1) * LANES)
                scr[k] = res[:, lanes]
                for c in range(n_cls):
                    cm_ref[c, :, lanes] = scr[
                        k, pl.ds(_class_residue(c), per_cls, stride=n_cls), :].astype(cm_ref.dtype)
        off += width
    kt_ref[...] = lax.dot_general(wkt_ref[...], h, (((1,), (1,)), ((), ())),
                                  preferred_element_type=F32).astype(kt_ref.dtype)


def _in_proj(x, mod_l, g_pre_l, w_cols, w_kt, widths, cm_groups):
    bn, s, d_model = x.shape
    tm = TILE_M
    n_t = s // tm
    n_cols = w_cols.shape[1]
    kt_rows = w_kt.shape[0]
    per_cls = tm // N_CLASSES
    out_shape = [jax.ShapeDtypeStruct((bn, s, w), BF16) for w in widths]
    out_specs = [pl.BlockSpec((None, tm, w), lambda b, i: (b, i, 0)) for w in widths]
    for gi in cm_groups:
        out_shape.append(jax.ShapeDtypeStruct((bn, N_CLASSES, s // N_CLASSES, widths[gi]), BF16))
        out_specs.append(pl.BlockSpec((None, N_CLASSES, per_cls, widths[gi]),
                                      lambda b, i: (b, 0, i, 0)))
    out_shape.append(jax.ShapeDtypeStruct((bn, n_t, kt_rows, tm), BF16))
    out_specs.append(pl.BlockSpec((None, None, kt_rows, tm), lambda b, i: (b, i, 0, 0)))
    return pl.pallas_call(
        functools.partial(_in_proj_kernel, widths=widths, cm_groups=cm_groups),
        grid=(bn, n_t),
        in_specs=[pl.BlockSpec((None, tm, d_model), lambda b, i: (b, i, 0)),
                  pl.BlockSpec((None, 1, 3 * d_model), lambda b, i: (b, 0, 0)),
                  pl.BlockSpec((1, d_model), lambda b, i: (0, 0)),
                  pl.BlockSpec((d_model, n_cols), lambda b, i: (0, 0),
                               pipeline_mode=pl.Buffered(1)),
                  pl.BlockSpec((kt_rows, d_model), lambda b, i: (0, 0),
                               pipeline_mode=pl.Buffered(1))],
        out_specs=out_specs,
        out_shape=out_shape,
        scratch_shapes=[pltpu.VMEM((widths[gi] // LANES, tm, LANES), F32) for gi in cm_groups],
        compiler_params=pltpu.CompilerParams(
            dimension_semantics=("parallel", "parallel"), vmem_limit_bytes=VMEM_LIMIT),
        name="in_proj",
    )(x, mod_l.reshape(bn, 1, 3 * d_model), g_pre_l.reshape(1, d_model), w_cols, w_kt)


def _lane_masks():
    lane = lax.broadcasted_iota(jnp.int32, (1, LANES), 1)
    even = lane < HEAD_DIM
    m_even = jnp.where(even, 1.0, 0.0).astype(BF16)
    m_odd = jnp.where(even, 0.0, 1.0).astype(BF16)
    return even, m_even, m_odd


def _banded_kernel(*refs, n_pairs, pairs_per_kv, has_sink, pieces):
    it = iter(refs)
    sink_ref = next(it) if has_sink else None
    q_ref, kp_ref, kc_ref, vp_ref, vc_ref, bias_ref, o_ref = (next(it) for _ in range(7))
    lse_ref = None if has_sink else next(it)
    k_buf, v_buf = next(it), next(it)
    step = pl.program_id(2)
    if pieces == 1:
        nblk = q_ref.shape[0] // BLK
        k_buf[:BLK] = kp_ref[...]
        k_buf[BLK:] = kc_ref[...]
        v_buf[:BLK] = vp_ref[...]
        v_buf[BLK:] = vc_ref[...]
        q_src, o_dst, lse_dst = q_ref, o_ref, lse_ref
    else:
        q_src, o_dst, lse_dst = next(it), next(it), next(it)
        per = BLK // pieces
        nblk = q_ref.shape[1] // per
        for pc in range(pieces):
            k_buf[pc * per:(pc + 1) * per] = kp_ref[pc]
            v_buf[pc * per:(pc + 1) * per] = vp_ref[pc]
            for ib in range(nblk):
                dst = (ib * pieces + pc) * per
                src = slice(ib * per, (ib + 1) * per)
                q_src[dst:dst + per] = q_ref[pc, src]
                k_buf[BLK + dst:BLK + dst + per] = kc_ref[pc, src]
                v_buf[BLK + dst:BLK + dst + per] = vc_ref[pc, src]
    even, m_even, m_odd = _lane_masks()
    lane = lax.broadcasted_iota(jnp.int32, (BLK, LANES), 1)
    top_rows = lax.broadcasted_iota(jnp.int32, (2 * BLK, 1), 0) < BLK
    ones = jnp.ones((2 * BLK, LANES), BF16)

    def block(ib, carry):
        r0 = pl.multiple_of(ib * BLK, BLK)
        variant = jnp.where((step == 0) & (ib == 0), 0, 1)
        lse_tile = jnp.zeros((BLK, LANES), F32)
        for j in range(n_pairs):
            g = j // pairs_per_kv
            q2 = q_src[pl.ds(r0, BLK), j * LANES:(j + 1) * LANES]
            qs = jnp.concatenate([q2 * m_even, q2 * m_odd], axis=0)
            k2 = k_buf[pl.ds(r0, 2 * BLK), g * LANES:(g + 1) * LANES]
            v2 = v_buf[pl.ds(r0, 2 * BLK), g * LANES:(g + 1) * LANES]
            s = lax.dot_general(qs, k2, (((1,), (1,)), ((), ())), preferred_element_type=F32)
            s = s + bias_ref[variant, j * 2 * BLK:(j + 1) * 2 * BLK, :]
            m = jnp.max(s, axis=-1, keepdims=True)
            if has_sink:
                sk = jnp.where(top_rows, sink_ref[2 * j], sink_ref[2 * j + 1]) * LOG2E
                m = jnp.maximum(m, sk)
            e = jnp.exp2(s - m)
            if pairs_per_kv > 1:
                ox = jnp.dot(e.astype(BF16), jnp.concatenate([v2, ones], axis=1),
                             preferred_element_type=F32)
                ov, den = ox[:, :LANES], ox[:, LANES:]
            else:
                den = jnp.sum(e, axis=-1, keepdims=True)
                ov = jnp.dot(e.astype(BF16), v2, preferred_element_type=F32)
            if has_sink:
                den = den + jnp.exp2(sk - m)
            o2 = ov / den
            o_dst[pl.ds(r0, BLK), j * LANES:(j + 1) * LANES] = (
                jnp.where(even, o2[:BLK], o2[BLK:]).astype(o_dst.dtype))
            if lse_dst is not None:
                lse = m + jnp.log2(den)
                lse_tile = jnp.where(lane == 2 * j, lse[:BLK], lse_tile)
                lse_tile = jnp.where(lane == 2 * j + 1, lse[BLK:], lse_tile)
        if lse_dst is not None:
            lse_dst[pl.ds(r0, BLK), :] = lse_tile
        return carry

    lax.fori_loop(0, nblk, block, 0)

    if pieces > 1:
        for pc in range(pieces):
            for ib in range(nblk):
                dst = (ib * pieces + pc) * per
                o_ref[pc, ib * per:(ib + 1) * per] = o_dst[dst:dst + per]
                lse_ref[pc, ib * per:(ib + 1) * per] = lse_dst[dst:dst + per]


def _banded_attention(q, k, v, bias, pairs_per_kv, layout, sinks=None):
    wq, wkv = q.shape[-1], k.shape[-1]
    bn = q.shape[0]
    n_pairs = wq // LANES
    has_sink = sinks is not None
    if layout == "natural":
        seq = q.shape[1]
        nblk = min(BAND_BLOCKS, seq // BLK)
        rows = nblk * BLK
        grid = (bn, 1, seq // rows)
        cur = lambda b, r, n: (b, n, 0)
        prev = lambda b, r, n: (b, jnp.maximum(n * nblk - 1, 0), 0)
        cur_blk = lambda w: (None, rows, w)
        prev_blk = lambda w: (None, BLK, w)
        pieces = 1
    elif layout == "classes":
        n_cls, seq = q.shape[1], q.shape[2]
        nblk = min(BAND_BLOCKS, seq // BLK)
        rows = nblk * BLK
        grid = (bn, n_cls, seq // rows)
        cur = lambda b, r, n: (b, r, n, 0)
        prev = lambda b, r, n: (b, r, jnp.maximum(n * nblk - 1, 0), 0)
        cur_blk = lambda w: (None, None, rows, w)
        prev_blk = lambda w: (None, None, BLK, w)
        pieces = 1
    else:
        n_cls, seq = q.shape[1], q.shape[2]
        pieces = BAND_PIECES
        per = BLK // pieces
        nblk = pieces
        rows = BLK
        grid = (bn, n_cls // pieces, seq // rows)
        cur = lambda b, r, n: (b, r, n, 0)
        prev = lambda b, r, n: (b, r, jnp.maximum(n * (rows // per) - 1, 0), 0)
        cur_blk = lambda w: (None, pieces, rows, w)
        prev_blk = lambda w: (None, pieces, per, w)
    tot = nblk * BLK
    in_specs = [pl.BlockSpec(cur_blk(wq), cur),
                pl.BlockSpec(prev_blk(wkv), prev),
                pl.BlockSpec(cur_blk(wkv), cur),
                pl.BlockSpec(prev_blk(wkv), prev),
                pl.BlockSpec(cur_blk(wkv), cur),
                pl.BlockSpec(bias.shape, lambda b, r, n: (0, 0, 0))]
    args = [q, k, k, v, v, bias]
    out_specs = [pl.BlockSpec(cur_blk(wq), cur)]
    out_shape = [jax.ShapeDtypeStruct(q.shape, BF16)]
    scratch = [pltpu.VMEM((tot + BLK, wkv), BF16), pltpu.VMEM((tot + BLK, wkv), BF16)]
    if has_sink:
        in_specs.insert(0, pl.BlockSpec(memory_space=pltpu.SMEM))
        args.insert(0, sinks)
    else:
        out_specs.append(pl.BlockSpec(cur_blk(LANES), cur))
        out_shape.append(jax.ShapeDtypeStruct(q.shape[:-1] + (LANES,), F32))
    if pieces > 1:
        scratch += [pltpu.VMEM((tot, wq), BF16), pltpu.VMEM((tot, wq), BF16),
                    pltpu.VMEM((tot, LANES), F32)]
    outs = pl.pallas_call(
        functools.partial(_banded_kernel, n_pairs=n_pairs, pairs_per_kv=pairs_per_kv,
                          has_sink=has_sink, pieces=pieces),
        grid=grid,
        in_specs=in_specs,
        out_specs=out_specs,
        out_shape=out_shape,
        scratch_shapes=scratch,
        compiler_params=pltpu.CompilerParams(
            dimension_semantics=("parallel", "parallel", "arbitrary"),
            vmem_limit_bytes=VMEM_LIMIT),
        name="banded_sink" if has_sink else f"banded_{layout}",
    )(*args)
    return outs[0] if has_sink else tuple(outs)


def _diff_kernel(far_ref, q_ref, kt_ref, v_ref, bias_ref, lam_ref, g_ref, o_ref,
                 qs_s, vx_s, m_s, acc_s, *, lam_init, n_near):
    h = pl.program_id(1)
    i = pl.program_id(2)
    t = q_ref.shape[0]
    n_rep = t // LANES

    @pl.when(i == 0)
    def _():
        vx_s[:, :LANES] = v_ref[...]
        vx_s[:, LANES:] = jnp.ones((vx_s.shape[0], LANES), BF16)

    _, m_even, m_odd = _lane_masks()
    q2 = q_ref[...]
    qs_s[:t] = q2 * m_even
    qs_s[t:] = q2 * m_odd
    m_s[...] = jnp.full(m_s.shape, NEG, F32)
    acc_s[...] = jnp.zeros(acc_s.shape, F32)
    far_bias = far_ref[h] * LOG2E

    def sweep(j0, n_tiles, near):
        k0 = pl.multiple_of(j0 * t, t)
        kts = [kt_ref[j0 + a] for a in range(n_tiles)]
        vx = vx_s[pl.ds(k0, n_tiles * t), :]
        for r0 in range(0, 2 * t, DIFF_ROWS):
            rows = pl.ds(r0, DIFF_ROWS)
            qc = qs_s[rows, :]
            parts = [jnp.dot(qc, kt, preferred_element_type=F32) for kt in kts]
            if near:
                parts = [part + bias_ref[jnp.minimum(i - j0 - a, n_near),
                                         pl.ds(r0 % t, DIFF_ROWS), :]
                         for a, part in enumerate(parts)]
                shift = 0.0
            else:
                shift = far_bias
            s = parts[0] if n_tiles == 1 else jnp.concatenate(parts, axis=1)
            m_cur = jnp.max(s, axis=-1, keepdims=True) + shift
            m_old = m_s[rows, :]
            m_new = jnp.maximum(m_old, m_cur)
            alpha = jnp.exp2(m_old - m_new)
            m_sub = m_new - shift
            p = jnp.exp2(s - jnp.concatenate([m_sub] * (n_tiles * n_rep), axis=1))
            pv = jnp.dot(p.astype(BF16), vx, preferred_element_type=F32)
            acc_s[rows, :] = jnp.concatenate([alpha, alpha], axis=1) * acc_s[rows, :] + pv
            m_s[rows, :] = m_new

    def far_pair(jj, carry):
        sweep(2 * jj, 2, False)
        return carry

    n_far_pairs = jnp.maximum(i - (n_near - 1), 0) >> 1
    j_near = 2 * n_far_pairs
    odd = (i + 1 - j_near) & 1
    lax.fori_loop(0, n_far_pairs, far_pair, 0)

    @pl.when(odd == 1)
    def _():
        sweep(j_near, 1, True)

    def near_pair(kk, carry):
        sweep(j_near + odd + 2 * kk, 2, True)
        return carry

    lax.fori_loop(0, (i + 1 - j_near) >> 1, near_pair, 0)

    acc = acc_s[...]
    o = acc[:, :LANES] / acc[:, LANES:]
    lam = (jnp.exp(jnp.sum(lam_ref[0:1, :] * lam_ref[1:2, :], axis=-1, keepdims=True))
           - jnp.exp(jnp.sum(lam_ref[2:3, :] * lam_ref[3:4, :], axis=-1, keepdims=True)) + lam_init)
    y = o[:t] - lam * o[t:]
    y = y * lax.rsqrt(jnp.mean(y * y, axis=-1, keepdims=True) + EPS)
    o_ref[...] = (y * g_ref[...] * (1.0 - lam_init)).astype(o_ref.dtype)


def _diff_attention(q, kt, v, bias_c, far_bias, lam_params, g_sub_l, lam_init):
    bn, s, w = q.shape
    t = TILE_M
    n_t = s // t
    n_heads = w // LANES
    n_near = bias_c.shape[1] - 1
    return pl.pallas_call(
        functools.partial(_diff_kernel, lam_init=lam_init, n_near=n_near),
        grid=(bn, n_heads, n_t),
        in_specs=[pl.BlockSpec(memory_space=pltpu.SMEM),
                  pl.BlockSpec((None, t, LANES), lambda b, h, i: (b, i, h)),
                  pl.BlockSpec((None, n_t, LANES, t), lambda b, h, i: (b, 0, h, 0)),
                  pl.BlockSpec((None, s, LANES), lambda b, h, i: (b, 0, h)),
                  pl.BlockSpec((None, n_near + 1, t, t), lambda b, h, i: (h, 0, 0, 0)),
                  pl.BlockSpec((4, HEAD_DIM), lambda b, h, i: (0, 0)),
                  pl.BlockSpec((1, LANES), lambda b, h, i: (0, 0))],
        out_specs=pl.BlockSpec((None, t, LANES), lambda b, h, i: (b, i, h)),
        out_shape=jax.ShapeDtypeStruct((bn, s, w), BF16),
        scratch_shapes=[pltpu.VMEM((2 * t, LANES), BF16),
                        pltpu.VMEM((s, 2 * LANES), BF16),
                        pltpu.VMEM((2 * t, LANES), F32),
                        pltpu.VMEM((2 * t, 2 * LANES), F32)],
        compiler_params=pltpu.CompilerParams(
            dimension_semantics=("parallel", "parallel", "arbitrary"),
            vmem_limit_bytes=VMEM_LIMIT),
        name="diff_attn",
    )(far_bias, q, kt, v, bias_c, lam_params, g_sub_l.reshape(1, LANES))


def _out_proj_kernel(x_ref, mod_ref, g_ref, w_ref, e_ref, ya_ref, o1_ref, o4_ref, o16_ref,
                     l1_ref, l4_ref, l16_ref, yc_ref, z_ref, out_ref,
                     o4_scr, o16_scr, l4_scr, l16_scr):
    d_model = x_ref.shape[-1]
    wa = ya_ref.shape[-1]
    wb = o1_ref.shape[-1]
    z = z_ref[...].astype(F32)
    sz = z / (1.0 + jnp.exp(-z))

    def to_natural(cm_ref, scr):
        n_cls, per_cls = cm_ref.shape[0], cm_ref.shape[1]
        for k in range(scr.shape[0]):
            for c in range(n_cls):
                scr[k, pl.ds(_class_residue(c), per_cls, stride=n_cls), :] = cm_ref[
                    c, :, k * LANES:(k + 1) * LANES].astype(F32)
        return jnp.concatenate([scr[k] for k in range(scr.shape[0])], axis=1)

    o_parts = [o1_ref[...].astype(F32), to_natural(o4_ref, o4_scr), to_natural(o16_ref, o16_scr)]
    lses = [l1_ref[...], to_natural(l4_ref, l4_scr), to_natural(l16_ref, l16_scr)]
    mx = jnp.maximum(jnp.maximum(lses[0], lses[1]), lses[2])
    ws = [jnp.exp2(l - mx) for l in lses]
    tot = ws[0] + ws[1] + ws[2]
    yb = jnp.zeros((x_ref.shape[0], wb), F32)
    for wgt, o_part in zip(ws, o_parts):
        a_hi, a_lo = _split_bf16(wgt / tot)
        spread = (jnp.dot(a_hi, e_ref[...], preferred_element_type=F32)
                  + jnp.dot(a_lo, e_ref[...], preferred_element_type=F32))
        yb = yb + spread * o_part

    ga = (ya_ref[...].astype(F32) * sz[:, :wa]).astype(BF16)
    gb = (yb * sz[:, wa:wa + wb]).astype(BF16)
    gc = (yc_ref[...].astype(F32) * sz[:, wa + wb:]).astype(BF16)
    y = jnp.dot(ga, w_ref[:wa], preferred_element_type=F32)
    y += jnp.dot(gb, w_ref[wa:wa + wb], preferred_element_type=F32)
    y += jnp.dot(gc, w_ref[wa + wb:], preferred_element_type=F32)
    r = (y * lax.rsqrt(jnp.mean(y * y, axis=-1, keepdims=True) + EPS)) * g_ref[...]
    gate = mod_ref[:, 2 * d_model:]
    out_ref[...] = x_ref[...] + gate * r


def _out_proj(x, mod_l, g_post_l, w_out_l, spread, ya, ob, lb, yc, z):
    bn, s, d_model = x.shape
    tm = TILE_M
    per_cls = tm // N_CLASSES
    wb = ob[0].shape[-1]
    row = lambda w: pl.BlockSpec((None, tm, w), lambda b, i: (b, i, 0))
    cm = lambda w: pl.BlockSpec((None, N_CLASSES, per_cls, w), lambda b, i: (b, 0, i, 0))
    whole = lambda a: pl.BlockSpec(a.shape, lambda b, i: (0,) * a.ndim,
                                   pipeline_mode=pl.Buffered(1))
    g2 = g_post_l.reshape(1, d_model)
    return pl.pallas_call(
        _out_proj_kernel,
        grid=(bn, s // tm),
        in_specs=[row(d_model),
                  pl.BlockSpec((None, 1, 3 * d_model), lambda b, i: (b, 0, 0)),
                  whole(g2), whole(w_out_l), whole(spread),
                  row(ya.shape[-1]),
                  row(wb), cm(wb), cm(wb),
                  row(LANES), cm(LANES), cm(LANES),
                  row(yc.shape[-1]), row(z.shape[-1])],
        out_specs=row(d_model),
        out_shape=jax.ShapeDtypeStruct(x.shape, x.dtype),
        scratch_shapes=[pltpu.VMEM((wb // LANES, tm, LANES), F32),
                        pltpu.VMEM((wb // LANES, tm, LANES), F32),
                        pltpu.VMEM((1, tm, LANES), F32), pltpu.VMEM((1, tm, LANES), F32)],
        compiler_params=pltpu.CompilerParams(
            dimension_semantics=("parallel", "parallel"), vmem_limit_bytes=VMEM_LIMIT),
        name="out_proj",
    )(x, mod_l.reshape(bn, 1, 3 * d_model), g2, w_out_l, spread, ya, *ob, *lb, yc, z)


def _prepare_w_in(w, sizes, n_kv_a):
    offs = np.concatenate([[0], np.cumsum(sizes)])
    grp = lambda i: w[:, int(offs[i]):int(offs[i + 1])]
    aq, ak, av, bq, bk, bv, cq, ck, cv, z = [grp(i) for i in range(10)]

    def dup(m):
        parts = []
        for g in range(n_kv_a):
            head = m[:, g * HEAD_DIM:(g + 1) * HEAD_DIM]
            parts += [head, head]
        return jnp.concatenate(parts, axis=1)

    groups = [aq * Q_SCALE, dup(ak), dup(av), bq * Q_SCALE, bk, bv, cq * Q_SCALE, cv, z]
    widths = tuple(int(g.shape[1]) for g in groups)
    w_cols = jnp.concatenate(groups, axis=1).astype(BF16)
    return w_cols, ck.T.astype(BF16), widths


def kernel(x, c, rel_table, w_in, w_out, w_ada, b_ada, g_pre, g_post, a_sinks,
           lam_q1, lam_k1, lam_q2, lam_k2, g_sub):
    bn, s, d_model = x.shape
    depth = w_in.shape[0]
    d_mix = w_out.shape[1]
    a_width = b_width = 3 * d_mix // 8
    c_width = d_mix // 4
    n_a_heads = a_width // HEAD_DIM
    n_kv_a = n_a_heads // 4
    n_b_heads = b_width // HEAD_DIM
    n_c_heads = c_width // (2 * HEAD_DIM)
    sizes = (a_width, n_kv_a * HEAD_DIM, n_kv_a * HEAD_DIM, b_width, b_width, b_width,
             c_width, c_width, c_width, d_mix)
    assert w_in.shape[2] == sum(sizes) and rel_table.shape == (REL_BUCKETS,
                                                                n_a_heads + n_b_heads + n_c_heads)
    assert n_a_heads == n_b_heads and s % (N_CLASSES * BLK) == 0
    assert [d for _, d in B_PATTERNS] == [1, BAND_PIECES, N_CLASSES]
    assert s % TILE_M == 0 and TILE_M % BLK == 0

    n_near = -(-(FAR_DIST - 1) // TILE_M) + 1
    bias_a, bias_b, bias_c = _bias_tables(rel_table, n_a_heads, n_b_heads, n_c_heads,
                                          TILE_M, n_near + 1)
    far_bias = rel_table[REL_BUCKETS - 1, n_a_heads + n_b_heads:]
    mod = _modulation(c, w_ada, b_ada)

    head_of_lane = np.arange(b_width) // HEAD_DIM
    spread = jnp.asarray(np.arange(LANES)[:, None] == head_of_lane[None, :], BF16)

    for l in range(depth):
        w_cols, w_kt, widths = _prepare_w_in(w_in[l], sizes, n_kv_a)
        (aq, ak, av, bq, bk, bv, cq, cv, z, bq_cm, bk_cm, bv_cm, ckt) = _in_proj(
            x, mod[l], g_pre[l], w_cols, w_kt, widths, cm_groups=(3, 4, 5))

        ya = _banded_attention(aq, ak, av, bias_a, 2, "natural", sinks=a_sinks[l])
        o1, l1 = _banded_attention(bq, bk, bv, bias_b[0], 1, "natural")
        o4, l4 = _banded_attention(bq_cm, bk_cm, bv_cm, bias_b[1], 1, "pieces")
        o16, l16 = _banded_attention(bq_cm, bk_cm, bv_cm, bias_b[2], 1, "classes")
        ob, lb = (o1, o4, o16), (l1, l4, l16)

        lam_init = 0.8 - 0.6 * math.exp(-0.3 * l)
        lam_params = jnp.stack([lam_q1[l], lam_k1[l], lam_q2[l], lam_k2[l]]).astype(F32)
        yc = _diff_attention(cq, ckt, cv, bias_c, far_bias, lam_params, g_sub[l], lam_init)

        x = _out_proj(x, mod[l], g_post[l], w_out[l].astype(BF16), spread, ya, ob, lb, yc, z)
    return x
```

```python
import functools
import math

import numpy as np
import jax
import jax.numpy as jnp
from jax import lax
from jax.experimental import pallas as pl
from jax.experimental.pallas import tpu as pltpu

F32 = jnp.float32
BF16 = jnp.bfloat16

HEAD_DIM = 64
LANES = 128
BLK = 128
A_WINDOW = 128
B_PATTERNS = ((128, 1), (512, 4), (2048, 16))
REL_BUCKETS = 32
REL_MAX_DIST = 2048
EPS = 1e-6
NEG = -1e30
LOG2E = 1.4426950408889634
Q_SCALE = LOG2E / math.sqrt(HEAD_DIM)

TILE_M = 512
BAND_BLOCKS = 4
N_CLASSES = 16
BAND_PIECES = 4
PIECED_PATTERN = 1
DIFF_ROWS = 128
DIFF_MAX_TILES = 4
VMEM_LIMIT = 56 * 1024 * 1024


def _class_residue(c):
    return c // BAND_PIECES + (N_CLASSES // BAND_PIECES) * (c % BAND_PIECES)


def _rel_bucket_np(n):
    n = np.maximum(np.asarray(n, np.int64), 0)
    max_exact = REL_BUCKETS // 2
    nf = np.maximum(n, 1).astype(np.float32)
    large = max_exact + (np.log(nf / np.float32(max_exact))
                         / np.float32(math.log(REL_MAX_DIST / max_exact))
                         * np.float32(REL_BUCKETS - max_exact)).astype(np.int32)
    large = np.minimum(large, REL_BUCKETS - 1)
    return np.where(n < max_exact, n, large)


_BUCKET_OF = _rel_bucket_np(np.arange(0, 1 << 18))
_THR = [0] + [int(np.argmax(_BUCKET_OF >= b)) for b in range(1, REL_BUCKETS)]
FAR_DIST = _THR[REL_BUCKETS - 1]


def _bucket_values(dist, dist_scale, lo, hi, table_ref, col):
    b_lo = int(_BUCKET_OF[max(lo, 0) * dist_scale])
    b_hi = int(_BUCKET_OF[max(hi, 0) * dist_scale])
    val = jnp.full(dist.shape, table_ref[b_lo, col] * LOG2E, F32)
    for b in range(b_lo + 1, b_hi + 1):
        thr = -(-_THR[b] // dist_scale)
        val = jnp.where(dist >= thr, table_ref[b, col] * LOG2E, val)
    return val


def _band_bias_kernel(table_ref, a_ref, b_ref):
    h = pl.program_id(0)
    n_a_heads = pl.num_programs(0)
    row = lax.broadcasted_iota(jnp.int32, (BLK, 2 * BLK), 0)
    col = lax.broadcasted_iota(jnp.int32, (BLK, 2 * BLK), 1)
    first_ok = col >= BLK

    def pieced(idx):
        per = BLK // BAND_PIECES
        return (idx & (per - 1)) * BAND_PIECES + (idx >> (per.bit_length() - 1))

    def tile(dist, max_dist, dist_scale, table_col):
        val = _bucket_values(dist, dist_scale, 0, max_dist, table_ref, table_col)
        val = jnp.where((dist >= 0) & (dist <= max_dist), val, NEG)
        return jnp.where(first_ok, val, NEG), val

    dist = row + BLK - col
    a_ref[0], a_ref[1] = tile(dist, A_WINDOW - 1, 1, h)
    for p, (w, d) in enumerate(B_PATTERNS):
        if p == PIECED_PATTERN:
            within = col & (BLK - 1)
            dist_p = pieced(row) + BLK - (pieced(within) + (col - within))
        else:
            dist_p = dist
        b_ref[p, 0], b_ref[p, 1] = tile(dist_p, w // d, d, n_a_heads + h)


def _diff_bias_kernel(table_ref, out_ref, *, tile, n_tiles, head_offset):
    h = pl.program_id(0)
    row = lax.broadcasted_iota(jnp.int32, (BLK, BLK), 0)
    col = lax.broadcasted_iota(jnp.int32, (BLK, BLK), 1)
    sub = tile // BLK
    pieces = {}
    for delta in range(-(sub - 1), n_tiles * sub):
        if delta < 0:
            pieces[delta] = jnp.full((BLK, BLK), NEG, F32)
            continue
        dist = row - col + delta * BLK
        val = _bucket_values(dist, 1, delta * BLK - (BLK - 1), delta * BLK + BLK - 1,
                             table_ref, head_offset + h)
        pieces[delta] = jnp.where(dist >= 0, val, NEG) if delta == 0 else val
    for dt in range(n_tiles):
        for a in range(sub):
            for b in range(sub):
                out_ref[dt, a * BLK:(a + 1) * BLK, b * BLK:(b + 1) * BLK] = pieces[dt * sub + a - b]


def _bias_tables(rel_table, n_a_heads, n_b_heads, n_c_heads, tile, n_tiles):
    smem = pl.BlockSpec(memory_space=pltpu.SMEM)
    bias_a, bias_b = pl.pallas_call(
        _band_bias_kernel,
        grid=(n_a_heads,),
        in_specs=[smem],
        out_specs=[pl.BlockSpec((2, BLK, 2 * BLK), lambda h: (0, h, 0)),
                   pl.BlockSpec((3, 2, BLK, 2 * BLK), lambda h: (0, 0, h, 0))],
        out_shape=[jax.ShapeDtypeStruct((2, n_a_heads * BLK, 2 * BLK), F32),
                   jax.ShapeDtypeStruct((3, 2, n_b_heads * BLK, 2 * BLK), F32)],
        name="band_bias",
    )(rel_table)
    bias_c = pl.pallas_call(
        functools.partial(_diff_bias_kernel, tile=tile, n_tiles=n_tiles,
                          head_offset=n_a_heads + n_b_heads),
        grid=(n_c_heads,),
        in_specs=[smem],
        out_specs=pl.BlockSpec((None, n_tiles, tile, tile), lambda h: (h, 0, 0, 0)),
        out_shape=jax.ShapeDtypeStruct((n_c_heads, n_tiles, tile, tile), F32),
        compiler_params=pltpu.CompilerParams(vmem_limit_bytes=VMEM_LIMIT),
        name="diff_bias",
    )(rel_table)
    return bias_a, bias_b, bias_c


def _split_bf16(v):
    hi = v.astype(BF16)
    lo = (v - hi.astype(F32)).astype(BF16)
    return hi, lo


def _mod_kernel(c_ref, w_ref, b_ref, out_ref):
    c = c_ref[...]
    s = c / (1.0 + jnp.exp(-c))
    s_hi, s_lo = _split_bf16(s)
    w_hi, w_lo = _split_bf16(w_ref[...])
    acc = jnp.dot(s_hi, w_hi, preferred_element_type=F32)
    acc += jnp.dot(s_lo, w_hi, preferred_element_type=F32)
    acc += jnp.dot(s_hi, w_lo, preferred_element_type=F32)
    out_ref[...] = acc + b_ref[...]


def _modulation(c, w_ada, b_ada):
    depth, d_model, n_out = w_ada.shape
    rows = 8
    c_pad = jnp.zeros((rows, d_model), F32).at[:c.shape[0]].set(c)
    tn = 768
    out = pl.pallas_call(
        _mod_kernel,
        grid=(depth, n_out // tn),
        in_specs=[pl.BlockSpec((rows, d_model), lambda l, j: (0, 0)),
                  pl.BlockSpec((None, d_model, tn), lambda l, j: (l, 0, j)),
                  pl.BlockSpec((None, 1, tn), lambda l, j: (l, 0, j))],
        out_specs=pl.BlockSpec((None, rows, tn), lambda l, j: (l, 0, j)),
        out_shape=jax.ShapeDtypeStruct((depth, rows, n_out), F32),
        name="adaln_mod",
    )(c_pad, w_ada, b_ada.reshape(depth, 1, n_out))
    return out[:, :c.shape[0]]


def _in_proj_kernel(x_ref, mod_ref, g_ref, w_ref, wkt_ref, *refs, widths, cm_groups):
    n_out = len(widths)
    col_refs = refs[:n_out]
    cm_refs = refs[n_out:n_out + len(cm_groups)]
    kt_ref = refs[n_out + len(cm_groups)]
    scr_refs = refs[n_out + len(cm_groups) + 1:]
    d_model = x_ref.shape[-1]
    x = x_ref[...]
    ms = jnp.mean(x * x, axis=-1, keepdims=True)
    shift = mod_ref[:, :d_model]
    scale = mod_ref[:, d_model:2 * d_model]
    h = (x * lax.rsqrt(ms + EPS)) * g_ref[...]
    h = (h * (1.0 + scale) + shift).astype(BF16)
    off = 0
    for gi, (ref, width) in enumerate(zip(col_refs, widths)):
        res = jnp.dot(h, w_ref[:, off:off + width], preferred_element_type=F32)
        ref[...] = res.astype(ref.dtype)
        if gi in cm_groups:
            cm_ref, scr = cm_refs[cm_groups.index(gi)], scr_refs[cm_groups.index(gi)]
            n_cls, per_cls = cm_ref.shape[0], cm_ref.shape[1]
            for k in range(width // LANES):
                lanes = slice(k * LANES, (k + 1) * LANES)
                scr[k] = res[:, lanes]
                for c in range(n_cls):
                    cm_ref[c, :, lanes] = scr[
                        k, pl.ds(_class_residue(c), per_cls, stride=n_cls), :].astype(cm_ref.dtype)
        off += width
    kt_ref[...] = lax.dot_general(wkt_ref[...], h, (((1,), (1,)), ((), ())),
                                  preferred_element_type=F32).astype(kt_ref.dtype)


def _in_proj(x, mod_l, g_pre_l, w_cols, w_kt, widths, cm_groups):
    bn, s, d_model = x.shape
    tm = TILE_M
    n_t = s // tm
    n_cols = w_cols.shape[1]
    kt_rows = w_kt.shape[0]
    per_cls = tm // N_CLASSES
    out_shape = [jax.ShapeDtypeStruct((bn, s, w), BF16) for w in widths]
    out_specs = [pl.BlockSpec((None, tm, w), lambda b, i: (b, i, 0)) for w in widths]
    for gi in cm_groups:
        out_shape.append(jax.ShapeDtypeStruct((bn, N_CLASSES, s // N_CLASSES, widths[gi]), BF16))
        out_specs.append(pl.BlockSpec((None, N_CLASSES, per_cls, widths[gi]),
                                      lambda b, i: (b, 0, i, 0)))
    out_shape.append(jax.ShapeDtypeStruct((bn, n_t, kt_rows, tm), BF16))
    out_specs.append(pl.BlockSpec((None, None, kt_rows, tm), lambda b, i: (b, i, 0, 0)))
    return pl.pallas_call(
        functools.partial(_in_proj_kernel, widths=widths, cm_groups=cm_groups),
        grid=(bn, n_t),
        in_specs=[pl.BlockSpec((None, tm, d_model), lambda b, i: (b, i, 0)),
                  pl.BlockSpec((None, 1, 3 * d_model), lambda b, i: (b, 0, 0)),
                  pl.BlockSpec((1, d_model), lambda b, i: (0, 0)),
                  pl.BlockSpec((d_model, n_cols), lambda b, i: (0, 0),
                               pipeline_mode=pl.Buffered(1)),
                  pl.BlockSpec((kt_rows, d_model), lambda b, i: (0, 0),
                               pipeline_mode=pl.Buffered(1))],
        out_specs=out_specs,
        out_shape=out_shape,
        scratch_shapes=[pltpu.VMEM((widths[gi] // LANES, tm, LANES), F32) for gi in cm_groups],
        compiler_params=pltpu.CompilerParams(
            dimension_semantics=("parallel", "parallel"), vmem_limit_bytes=VMEM_LIMIT),
        name="in_proj",
    )(x, mod_l.reshape(bn, 1, 3 * d_model), g_pre_l.reshape(1, d_model), w_cols, w_kt)


def _lane_masks():
    lane = lax.broadcasted_iota(jnp.int32, (1, LANES), 1)
    even = lane < HEAD_DIM
    m_even = jnp.where(even, 1.0, 0.0).astype(BF16)
    m_odd = jnp.where(even, 0.0, 1.0).astype(BF16)
    return even, m_even, m_odd


def _banded_kernel(*refs, n_pairs, pairs_per_kv, has_sink, pieces):
    it = iter(refs)
    sink_ref = next(it) if has_sink else None
    q_ref, kp_ref, kc_ref, vp_ref, vc_ref, bias_ref, o_ref = (next(it) for _ in range(7))
    lse_ref = None if has_sink else next(it)
    k_buf, v_buf, s_buf = next(it), next(it), next(it)
    step = pl.program_id(2)
    if pieces == 1:
        nblk = q_ref.shape[0] // BLK
        k_buf[:BLK] = kp_ref[...]
        k_buf[BLK:] = kc_ref[...]
        v_buf[:BLK] = vp_ref[...]
        v_buf[BLK:] = vc_ref[...]
        q_src, o_dst, lse_dst = q_ref, o_ref, lse_ref
    else:
        q_src, o_dst, lse_dst = next(it), next(it), next(it)
        per = BLK // pieces
        nblk = q_ref.shape[1] // per
        for pc in range(pieces):
            k_buf[pc * per:(pc + 1) * per] = kp_ref[pc]
            v_buf[pc * per:(pc + 1) * per] = vp_ref[pc]
            for ib in range(nblk):
                dst = (ib * pieces + pc) * per
                src = slice(ib * per, (ib + 1) * per)
                q_src[dst:dst + per] = q_ref[pc, src]
                k_buf[BLK + dst:BLK + dst + per] = kc_ref[pc, src]
                v_buf[BLK + dst:BLK + dst + per] = vc_ref[pc, src]
    even, m_even, m_odd = _lane_masks()
    lane = lax.broadcasted_iota(jnp.int32, (BLK, LANES), 1)
    top_rows = lax.broadcasted_iota(jnp.int32, (2 * BLK, 1), 0) < BLK
    ones = jnp.ones((2 * BLK, LANES), BF16)

    def block(ib, carry):
        r0 = pl.multiple_of(ib * BLK, BLK)
        variant = jnp.where((step == 0) & (ib == 0), 0, 1)
        lse_tile = jnp.zeros((BLK, LANES), F32)
        for j in range(n_pairs):
            g = j // pairs_per_kv
            q2 = q_src[pl.ds(r0, BLK), j * LANES:(j + 1) * LANES]
            qs = jnp.concatenate([q2 * m_even, q2 * m_odd], axis=0)
            k2 = k_buf[pl.ds(r0, 2 * BLK), g * LANES:(g + 1) * LANES]
            s = lax.dot_general(qs, k2, (((1,), (1,)), ((), ())), preferred_element_type=F32)
            s_buf[j] = s + bias_ref[variant, j * 2 * BLK:(j + 1) * 2 * BLK, :]
        for j in range(n_pairs):
            g = j // pairs_per_kv
            v2 = v_buf[pl.ds(r0, 2 * BLK), g * LANES:(g + 1) * LANES]
            s = s_buf[j]
            m = jnp.max(s, axis=-1, keepdims=True)
            if has_sink:
                sk = jnp.where(top_rows, sink_ref[2 * j], sink_ref[2 * j + 1]) * LOG2E
                m = jnp.maximum(m, sk)
            e = jnp.exp2(s - m)
            if pairs_per_kv > 1:
                ox = jnp.dot(e.astype(BF16), jnp.concatenate([v2, ones], axis=1),
                             preferred_element_type=F32)
                ov, den = ox[:, :LANES], ox[:, LANES:]
            else:
                den = jnp.sum(e, axis=-1, keepdims=True)
                ov = jnp.dot(e.astype(BF16), v2, preferred_element_type=F32)
            if has_sink:
                den = den + jnp.exp2(sk - m)
            o2 = ov / den
            o_dst[pl.ds(r0, BLK), j * LANES:(j + 1) * LANES] = (
                jnp.where(even, o2[:BLK], o2[BLK:]).astype(o_dst.dtype))
            if lse_dst is not None:
                lse = m + jnp.log2(den)
                lse_tile = jnp.where(lane == 2 * j, lse[:BLK], lse_tile)
                lse_tile = jnp.where(lane == 2 * j + 1, lse[BLK:], lse_tile)
        if lse_dst is not None:
            lse_dst[pl.ds(r0, BLK), :] = lse_tile
        return carry

    lax.fori_loop(0, nblk, block, 0)

    if pieces > 1:
        for pc in range(pieces):
            for ib in range(nblk):
                dst = (ib * pieces + pc) * per
                o_ref[pc, ib * per:(ib + 1) * per] = o_dst[dst:dst + per]
                lse_ref[pc, ib * per:(ib + 1) * per] = lse_dst[dst:dst + per]


def _banded_attention(q, k, v, bias, pairs_per_kv, layout, sinks=None):
    wq, wkv = q.shape[-1], k.shape[-1]
    bn = q.shape[0]
    n_pairs = wq // LANES
    has_sink = sinks is not None
    if layout == "natural":
        seq = q.shape[1]
        nblk = min(BAND_BLOCKS, seq // BLK)
        rows = nblk * BLK
        grid = (bn, 1, seq // rows)
        cur = lambda b, r, n: (b, n, 0)
        prev = lambda b, r, n: (b, jnp.maximum(n * nblk - 1, 0), 0)
        cur_blk = lambda w: (None, rows, w)
        prev_blk = lambda w: (None, BLK, w)
        pieces = 1
    elif layout == "classes":
        n_cls, seq = q.shape[1], q.shape[2]
        nblk = min(BAND_BLOCKS, seq // BLK)
        rows = nblk * BLK
        grid = (bn, n_cls, seq // rows)
        cur = lambda b, r, n: (b, r, n, 0)
        prev = lambda b, r, n: (b, r, jnp.maximum(n * nblk - 1, 0), 0)
        cur_blk = lambda w: (None, None, rows, w)
        prev_blk = lambda w: (None, None, BLK, w)
        pieces = 1
    else:
        n_cls, seq = q.shape[1], q.shape[2]
        pieces = BAND_PIECES
        per = BLK // pieces
        nblk = pieces
        rows = BLK
        grid = (bn, n_cls // pieces, seq // rows)
        cur = lambda b, r, n: (b, r, n, 0)
        prev = lambda b, r, n: (b, r, jnp.maximum(n * (rows // per) - 1, 0), 0)
        cur_blk = lambda w: (None, pieces, rows, w)
        prev_blk = lambda w: (None, pieces, per, w)
    tot = nblk * BLK
    in_specs = [pl.BlockSpec(cur_blk(wq), cur),
                pl.BlockSpec(prev_blk(wkv), prev),
                pl.BlockSpec(cur_blk(wkv), cur),
                pl.BlockSpec(prev_blk(wkv), prev),
                pl.BlockSpec(cur_blk(wkv), cur),
                pl.BlockSpec(bias.shape, lambda b, r, n: (0, 0, 0))]
    args = [q, k, k, v, v, bias]
    out_specs = [pl.BlockSpec(cur_blk(wq), cur)]
    out_shape = [jax.ShapeDtypeStruct(q.shape, BF16)]
    scratch = [pltpu.VMEM((tot + BLK, wkv), BF16), pltpu.VMEM((tot + BLK, wkv), BF16),
               pltpu.VMEM((n_pairs, 2 * BLK, 2 * BLK), F32)]
    if has_sink:
        in_specs.insert(0, pl.BlockSpec(memory_space=pltpu.SMEM))
        args.insert(0, sinks)
    else:
        out_specs.append(pl.BlockSpec(cur_blk(LANES), cur))
        out_shape.append(jax.ShapeDtypeStruct(q.shape[:-1] + (LANES,), F32))
    if pieces > 1:
        scratch += [pltpu.VMEM((tot, wq), BF16), pltpu.VMEM((tot, wq), BF16),
                    pltpu.VMEM((tot, LANES), F32)]
    outs = pl.pallas_call(
        functools.partial(_banded_kernel, n_pairs=n_pairs, pairs_per_kv=pairs_per_kv,
                          has_sink=has_sink, pieces=pieces),
        grid=grid,
        in_specs=in_specs,
        out_specs=out_specs,
        out_shape=out_shape,
        scratch_shapes=scratch,
        compiler_params=pltpu.CompilerParams(
            dimension_semantics=("parallel", "parallel", "arbitrary"),
            vmem_limit_bytes=VMEM_LIMIT),
        name="banded_sink" if has_sink else f"banded_{layout}",
    )(*args)
    return outs[0] if has_sink else tuple(outs)


def _diff_kernel(far_ref, q_ref, kt_ref, v_ref, bias_ref, lam_ref, g_ref, o_ref,
                 qs_s, vx_s, m_s, acc_s, s_s, *, lam_init, n_near):
    h = pl.program_id(1)
    i = pl.program_id(2)
    t = q_ref.shape[0]
    n_rep = t // LANES

    @pl.when(i == 0)
    def _():
        vx_s[:, :LANES] = v_ref[...]
        vx_s[:, LANES:] = jnp.ones((vx_s.shape[0], LANES), BF16)

    _, m_even, m_odd = _lane_masks()
    q2 = q_ref[...]
    qs_s[:t] = q2 * m_even
    qs_s[t:] = q2 * m_odd
    m_s[...] = jnp.full(m_s.shape, NEG, F32)
    acc_s[...] = jnp.zeros(acc_s.shape, F32)
    far_bias = far_ref[h] * LOG2E

    def sweep(j0, n_tiles, near):
        k0 = pl.multiple_of(j0 * t, t)
        kts = [kt_ref[j0 + a] for a in range(n_tiles)]
        vx = vx_s[pl.ds(k0, n_tiles * t), :]
        width = n_tiles * t
        for r0 in range(0, 2 * t, DIFF_ROWS):
            rows = pl.ds(r0, DIFF_ROWS)
            qc = qs_s[rows, :]
            for a, kt in enumerate(kts):
                part = jnp.dot(qc, kt, preferred_element_type=F32)
                if near:
                    part = part + bias_ref[jnp.minimum(i - j0 - a, n_near),
                                           pl.ds(r0 % t, DIFF_ROWS), :]
                s_s[rows, a * t:(a + 1) * t] = part
        shift = 0.0 if near else far_bias
        for r0 in range(0, 2 * t, DIFF_ROWS):
            rows = pl.ds(r0, DIFF_ROWS)
            s = s_s[rows, :width]
            m_cur = jnp.max(s, axis=-1, keepdims=True) + shift
            m_old = m_s[rows, :]
            m_new = jnp.maximum(m_old, m_cur)
            alpha = jnp.exp2(m_old - m_new)
            m_sub = m_new - shift
            p = jnp.exp2(s - jnp.concatenate([m_sub] * (n_tiles * n_rep), axis=1))
            pv = jnp.dot(p.astype(BF16), vx, preferred_element_type=F32)
            acc_s[rows, :] = jnp.concatenate([alpha, alpha], axis=1) * acc_s[rows, :] + pv
            m_s[rows, :] = m_new

    n_far = jnp.maximum(i - (n_near - 1), 0)
    n_quads = n_far // DIFF_MAX_TILES
    j_pairs = DIFF_MAX_TILES * n_quads
    n_far_pairs = (n_far - j_pairs) >> 1
    j_near = j_pairs + 2 * n_far_pairs
    odd = (i + 1 - j_near) & 1

    def far_quad(jj, carry):
        sweep(DIFF_MAX_TILES * jj, DIFF_MAX_TILES, False)
        return carry

    def far_pair(jj, carry):
        sweep(j_pairs + 2 * jj, 2, False)
        return carry

    lax.fori_loop(0, n_quads, far_quad, 0)
    lax.fori_loop(0, n_far_pairs, far_pair, 0)

    @pl.when(odd == 1)
    def _():
        sweep(j_near, 1, True)

    def near_pair(kk, carry):
        sweep(j_near + odd + 2 * kk, 2, True)
        return carry

    lax.fori_loop(0, (i + 1 - j_near) >> 1, near_pair, 0)

    acc = acc_s[...]
    o = acc[:, :LANES] / acc[:, LANES:]
    lam = (jnp.exp(jnp.sum(lam_ref[0:1, :] * lam_ref[1:2, :], axis=-1, keepdims=True))
           - jnp.exp(jnp.sum(lam_ref[2:3, :] * lam_ref[3:4, :], axis=-1, keepdims=True)) + lam_init)
    y = o[:t] - lam * o[t:]
    y = y * lax.rsqrt(jnp.mean(y * y, axis=-1, keepdims=True) + EPS)
    o_ref[...] = (y * g_ref[...] * (1.0 - lam_init)).astype(o_ref.dtype)


def _diff_attention(q, kt, v, bias_c, far_bias, lam_params, g_sub_l, lam_init):
    bn, s, w = q.shape
    t = TILE_M
    n_t = s // t
    n_heads = w // LANES
    n_near = bias_c.shape[1] - 1
    return pl.pallas_call(
        functools.partial(_diff_kernel, lam_init=lam_init, n_near=n_near),
        grid=(bn, n_heads, n_t),
        in_specs=[pl.BlockSpec(memory_space=pltpu.SMEM),
                  pl.BlockSpec((None, t, LANES), lambda b, h, i: (b, i, h)),
                  pl.BlockSpec((None, n_t, LANES, t), lambda b, h, i: (b, 0, h, 0)),
                  pl.BlockSpec((None, s, LANES), lambda b, h, i: (b, 0, h)),
                  pl.BlockSpec((None, n_near + 1, t, t), lambda b, h, i: (h, 0, 0, 0)),
                  pl.BlockSpec((4, HEAD_DIM), lambda b, h, i: (0, 0)),
                  pl.BlockSpec((1, LANES), lambda b, h, i: (0, 0))],
        out_specs=pl.BlockSpec((None, t, LANES), lambda b, h, i: (b, i, h)),
        out_shape=jax.ShapeDtypeStruct((bn, s, w), BF16),
        scratch_shapes=[pltpu.VMEM((2 * t, LANES), BF16),
                        pltpu.VMEM((s, 2 * LANES), BF16),
                        pltpu.VMEM((2 * t, LANES), F32),
                        pltpu.VMEM((2 * t, 2 * LANES), F32),
                        pltpu.VMEM((2 * t, DIFF_MAX_TILES * t), F32)],
        compiler_params=pltpu.CompilerParams(
            dimension_semantics=("parallel", "parallel", "arbitrary"),
            vmem_limit_bytes=VMEM_LIMIT),
        name="diff_attn",
    )(far_bias, q, kt, v, bias_c, lam_params, g_sub_l.reshape(1, LANES))


def _out_proj_kernel(x_ref, mod_ref, g_ref, w_ref, e_ref, ya_ref, o1_ref, o4_ref, o16_ref,
                     l1_ref, l4_ref, l16_ref, yc_ref, z_ref, out_ref,
                     o4_scr, o16_scr, l4_scr, l16_scr):
    d_model = x_ref.shape[-1]
    wa = ya_ref.shape[-1]
    wb = o1_ref.shape[-1]
    z = z_ref[...].astype(F32)
    sz = z / (1.0 + jnp.exp(-z))

    def to_natural(cm_ref, scr):
        n_cls, per_cls = cm_ref.shape[0], cm_ref.shape[1]
        for k in range(scr.shape[0]):
            for c in range(n_cls):
                scr[k, pl.ds(_class_residue(c), per_cls, stride=n_cls), :] = cm_ref[
                    c, :, k * LANES:(k + 1) * LANES].astype(F32)
        return jnp.concatenate([scr[k] for k in range(scr.shape[0])], axis=1)

    o_parts = [o1_ref[...].astype(F32), to_natural(o4_ref, o4_scr), to_natural(o16_ref, o16_scr)]
    lses = [l1_ref[...], to_natural(l4_ref, l4_scr), to_natural(l16_ref, l16_scr)]
    mx = jnp.maximum(jnp.maximum(lses[0], lses[1]), lses[2])
    ws = [jnp.exp2(l - mx) for l in lses]
    tot = ws[0] + ws[1] + ws[2]
    yb = jnp.zeros((x_ref.shape[0], wb), F32)
    for wgt, o_part in zip(ws, o_parts):
        a_hi, a_lo = _split_bf16(wgt / tot)
        spread = (jnp.dot(a_hi, e_ref[...], preferred_element_type=F32)
                  + jnp.dot(a_lo, e_ref[...], preferred_element_type=F32))
        yb = yb + spread * o_part

    ga = (ya_ref[...].astype(F32) * sz[:, :wa]).astype(BF16)
    gb = (yb * sz[:, wa:wa + wb]).astype(BF16)
    gc = (yc_ref[...].astype(F32) * sz[:, wa + wb:]).astype(BF16)
    y = jnp.dot(ga, w_ref[:wa], preferred_element_type=F32)
    y += jnp.dot(gb, w_ref[wa:wa + wb], preferred_element_type=F32)
    y += jnp.dot(gc, w_ref[wa + wb:], preferred_element_type=F32)
    r = (y * lax.rsqrt(jnp.mean(y * y, axis=-1, keepdims=True) + EPS)) * g_ref[...]
    gate = mod_ref[:, 2 * d_model:]
    out_ref[...] = x_ref[...] + gate * r


def _out_proj(x, mod_l, g_post_l, w_out_l, spread, ya, ob, lb, yc, z):
    bn, s, d_model = x.shape
    tm = TILE_M
    per_cls = tm // N_CLASSES
    wb = ob[0].shape[-1]
    row = lambda w: pl.BlockSpec((None, tm, w), lambda b, i: (b, i, 0))
    cm = lambda w: pl.BlockSpec((None, N_CLASSES, per_cls, w), lambda b, i: (b, 0, i, 0))
    whole = lambda a: pl.BlockSpec(a.shape, lambda b, i: (0,) * a.ndim,
                                   pipeline_mode=pl.Buffered(1))
    g2 = g_post_l.reshape(1, d_model)
    return pl.pallas_call(
        _out_proj_kernel,
        grid=(bn, s // tm),
        in_specs=[row(d_model),
                  pl.BlockSpec((None, 1, 3 * d_model), lambda b, i: (b, 0, 0)),
                  whole(g2), whole(w_out_l), whole(spread),
                  row(ya.shape[-1]),
                  row(wb), cm(wb), cm(wb),
                  row(LANES), cm(LANES), cm(LANES),
                  row(yc.shape[-1]), row(z.shape[-1])],
        out_specs=row(d_model),
        out_shape=jax.ShapeDtypeStruct(x.shape, x.dtype),
        scratch_shapes=[pltpu.VMEM((wb // LANES, tm, LANES), F32),
                        pltpu.VMEM((wb // LANES, tm, LANES), F32),
                        pltpu.VMEM((1, tm, LANES), F32), pltpu.VMEM((1, tm, LANES), F32)],
        compiler_params=pltpu.CompilerParams(
            dimension_semantics=("parallel", "parallel"), vmem_limit_bytes=VMEM_LIMIT),
        name="out_proj",
    )(x, mod_l.reshape(bn, 1, 3 * d_model), g2, w_out_l, spread, ya, *ob, *lb, yc, z)


def _prepare_w_in(w, sizes, n_kv_a):
    offs = np.concatenate([[0], np.cumsum(sizes)])
    grp = lambda i: w[:, int(offs[i]):int(offs[i + 1])]
    aq, ak, av, bq, bk, bv, cq, ck, cv, z = [grp(i) for i in range(10)]

    def dup(m):
        parts = []
        for g in range(n_kv_a):
            head = m[:, g * HEAD_DIM:(g + 1) * HEAD_DIM]
            parts += [head, head]
        return jnp.concatenate(parts, axis=1)

    groups = [aq * Q_SCALE, dup(ak), dup(av), bq * Q_SCALE, bk, bv, cq * Q_SCALE, cv, z]
    widths = tuple(int(g.shape[1]) for g in groups)
    w_cols = jnp.concatenate(groups, axis=1).astype(BF16)
    return w_cols, ck.T.astype(BF16), widths


def kernel(x, c, rel_table, w_in, w_out, w_ada, b_ada, g_pre, g_post, a_sinks,
           lam_q1, lam_k1, lam_q2, lam_k2, g_sub):
    bn, s, d_model = x.shape
    depth = w_in.shape[0]
    d_mix = w_out.shape[1]
    a_width = b_width = 3 * d_mix // 8
    c_width = d_mix // 4
    n_a_heads = a_width // HEAD_DIM
    n_kv_a = n_a_heads // 4
    n_b_heads = b_width // HEAD_DIM
    n_c_heads = c_width // (2 * HEAD_DIM)
    sizes = (a_width, n_kv_a * HEAD_DIM, n_kv_a * HEAD_DIM, b_width, b_width, b_width,
             c_width, c_width, c_width, d_mix)
    assert w_in.shape[2] == sum(sizes) and rel_table.shape == (REL_BUCKETS,
                                                                n_a_heads + n_b_heads + n_c_heads)
    assert n_a_heads == n_b_heads and s % (N_CLASSES * BLK) == 0
    assert [d for _, d in B_PATTERNS] == [1, BAND_PIECES, N_CLASSES]
    assert s % TILE_M == 0 and TILE_M % BLK == 0

    n_near = -(-(FAR_DIST - 1) // TILE_M) + 1
    bias_a, bias_b, bias_c = _bias_tables(rel_table, n_a_heads, n_b_heads, n_c_heads,
                                          TILE_M, n_near + 1)
    far_bias = rel_table[REL_BUCKETS - 1, n_a_heads + n_b_heads:]
    mod = _modulation(c, w_ada, b_ada)

    head_of_lane = np.arange(b_width) // HEAD_DIM
    spread = jnp.asarray(np.arange(LANES)[:, None] == head_of_lane[None, :], BF16)

    for l in range(depth):
        w_cols, w_kt, widths = _prepare_w_in(w_in[l], sizes, n_kv_a)
        (aq, ak, av, bq, bk, bv, cq, cv, z, bq_cm, bk_cm, bv_cm, ckt) = _in_proj(
            x, mod[l], g_pre[l], w_cols, w_kt, widths, cm_groups=(3, 4, 5))

        ya = _banded_attention(aq, ak, av, bias_a, 2, "natural", sinks=a_sinks[l])
        o1, l1 = _banded_attention(bq, bk, bv, bias_b[0], 1, "natural")
        o4, l4 = _banded_attention(bq_cm, bk_cm, bv_cm, bias_b[1], 1, "pieces")
        o16, l16 = _banded_attention(bq_cm, bk_cm, bv_cm, bias_b[2], 1, "classes")
        ob, lb = (o1, o4, o16), (l1, l4, l16)

        lam_init = 0.8 - 0.6 * math.exp(-0.3 * l)
        lam_params = jnp.stack([lam_q1[l], lam_k1[l], lam_q2[l], lam_k2[l]]).astype(F32)
        yc = _diff_attention(cq, ckt, cv, bias_c, far_bias, lam_params, g_sub[l], lam_init)

        x = _out_proj(x, mod[l], g_post[l], w_out[l].astype(BF16), spread, ya, ob, lb, yc, z)
    return x
```

```python
import functools
import math

import numpy as np
import jax
import jax.numpy as jnp
from jax import lax
from jax.experimental import pallas as pl
from jax.experimental.pallas import tpu as pltpu

F32 = jnp.float32
BF16 = jnp.bfloat16

HEAD_DIM = 64
LANES = 128
BLK = 128
A_WINDOW = 128
B_PATTERNS = ((128, 1), (512, 4), (2048, 16))
REL_BUCKETS = 32
REL_MAX_DIST = 2048
EPS = 1e-6
NEG = -1e30
LOG2E = 1.4426950408889634
Q_SCALE = LOG2E / math.sqrt(HEAD_DIM)

TILE_M = 512
BAND_BLOCKS = 4
N_CLASSES = 16
BAND_PIECES = 4
PIECED_PATTERN = 1
DIFF_ROWS = 128
DIFF_MAX_TILES = 4
VMEM_LIMIT = 56 * 1024 * 1024


def _class_residue(c):
    return c // BAND_PIECES + (N_CLASSES // BAND_PIECES) * (c % BAND_PIECES)


def _rel_bucket_np(n):
    n = np.maximum(np.asarray(n, np.int64), 0)
    max_exact = REL_BUCKETS // 2
    nf = np.maximum(n, 1).astype(np.float32)
    large = max_exact + (np.log(nf / np.float32(max_exact))
                         / np.float32(math.log(REL_MAX_DIST / max_exact))
                         * np.float32(REL_BUCKETS - max_exact)).astype(np.int32)
    large = np.minimum(large, REL_BUCKETS - 1)
    return np.where(n < max_exact, n, large)


_BUCKET_OF = _rel_bucket_np(np.arange(0, 1 << 18))
_THR = [0] + [int(np.argmax(_BUCKET_OF >= b)) for b in range(1, REL_BUCKETS)]
FAR_DIST = _THR[REL_BUCKETS - 1]


def _bucket_values(dist, dist_scale, lo, hi, table_ref, col):
    b_lo = int(_BUCKET_OF[max(lo, 0) * dist_scale])
    b_hi = int(_BUCKET_OF[max(hi, 0) * dist_scale])
    val = jnp.full(dist.shape, table_ref[b_lo, col] * LOG2E, F32)
    for b in range(b_lo + 1, b_hi + 1):
        thr = -(-_THR[b] // dist_scale)
        val = jnp.where(dist >= thr, table_ref[b, col] * LOG2E, val)
    return val


def _band_bias_kernel(table_ref, a_ref, b_ref):
    h = pl.program_id(0)
    n_a_heads = pl.num_programs(0)
    row = lax.broadcasted_iota(jnp.int32, (BLK, 2 * BLK), 0)
    col = lax.broadcasted_iota(jnp.int32, (BLK, 2 * BLK), 1)
    first_ok = col >= BLK

    def pieced(idx):
        per = BLK // BAND_PIECES
        return (idx & (per - 1)) * BAND_PIECES + (idx >> (per.bit_length() - 1))

    def tile(dist, max_dist, dist_scale, table_col):
        val = _bucket_values(dist, dist_scale, 0, max_dist, table_ref, table_col)
        val = jnp.where((dist >= 0) & (dist <= max_dist), val, NEG)
        return jnp.where(first_ok, val, NEG), val

    dist = row + BLK - col
    a_ref[0], a_ref[1] = tile(dist, A_WINDOW - 1, 1, h)
    for p, (w, d) in enumerate(B_PATTERNS):
        if p == PIECED_PATTERN:
            within = col & (BLK - 1)
            dist_p = pieced(row) + BLK - (pieced(within) + (col - within))
        else:
            dist_p = dist
        b_ref[p, 0], b_ref[p, 1] = tile(dist_p, w // d, d, n_a_heads + h)


def _diff_bias_kernel(table_ref, out_ref, *, tile, n_tiles, head_offset):
    h = pl.program_id(0)
    row = lax.broadcasted_iota(jnp.int32, (BLK, BLK), 0)
    col = lax.broadcasted_iota(jnp.int32, (BLK, BLK), 1)
    sub = tile // BLK
    pieces = {}
    for delta in range(-(sub - 1), n_tiles * sub):
        if delta < 0:
            pieces[delta] = jnp.full((BLK, BLK), NEG, F32)
            continue
        dist = row - col + delta * BLK
        val = _bucket_values(dist, 1, delta * BLK - (BLK - 1), delta * BLK + BLK - 1,
                             table_ref, head_offset + h)
        pieces[delta] = jnp.where(dist >= 0, val, NEG) if delta == 0 else val
    for dt in range(n_tiles):
        for a in range(sub):
            for b in range(sub):
                out_ref[dt, a * BLK:(a + 1) * BLK, b * BLK:(b + 1) * BLK] = pieces[dt * sub + a - b]


def _bias_tables(rel_table, n_a_heads, n_b_heads, n_c_heads, tile, n_tiles):
    smem = pl.BlockSpec(memory_space=pltpu.SMEM)
    bias_a, bias_b = pl.pallas_call(
        _band_bias_kernel,
        grid=(n_a_heads,),
        in_specs=[smem],
        out_specs=[pl.BlockSpec((2, BLK, 2 * BLK), lambda h: (0, h, 0)),
                   pl.BlockSpec((3, 2, BLK, 2 * BLK), lambda h: (0, 0, h, 0))],
        out_shape=[jax.ShapeDtypeStruct((2, n_a_heads * BLK, 2 * BLK), F32),
                   jax.ShapeDtypeStruct((3, 2, n_b_heads * BLK, 2 * BLK), F32)],
        name="band_bias",
    )(rel_table)
    bias_c = pl.pallas_call(
        functools.partial(_diff_bias_kernel, tile=tile, n_tiles=n_tiles,
                          head_offset=n_a_heads + n_b_heads),
        grid=(n_c_heads,),
        in_specs=[smem],
        out_specs=pl.BlockSpec((None, n_tiles, tile, tile), lambda h: (h, 0, 0, 0)),
        out_shape=jax.ShapeDtypeStruct((n_c_heads, n_tiles, tile, tile), F32),
        compiler_params=pltpu.CompilerParams(vmem_limit_bytes=VMEM_LIMIT),
        name="diff_bias",
    )(rel_table)
    return bias_a, bias_b, bias_c


def _split_bf16(v):
    hi = v.astype(BF16)
    lo = (v - hi.astype(F32)).astype(BF16)
    return hi, lo


def _mod_kernel(c_ref, w_ref, b_ref, out_ref):
    c = c_ref[...]
    s = c / (1.0 + jnp.exp(-c))
    s_hi, s_lo = _split_bf16(s)
    w_hi, w_lo = _split_bf16(w_ref[...])
    acc = jnp.dot(s_hi, w_hi, preferred_element_type=F32)
    acc += jnp.dot(s_lo, w_hi, preferred_element_type=F32)
    acc += jnp.dot(s_hi, w_lo, preferred_element_type=F32)
    out_ref[...] = acc + b_ref[...]


def _modulation(c, w_ada, b_ada):
    depth, d_model, n_out = w_ada.shape
    rows = 8
    c_pad = jnp.zeros((rows, d_model), F32).at[:c.shape[0]].set(c)
    tn = 768
    out = pl.pallas_call(
        _mod_kernel,
        grid=(depth, n_out // tn),
        in_specs=[pl.BlockSpec((rows, d_model), lambda l, j: (0, 0)),
                  pl.BlockSpec((None, d_model, tn), lambda l, j: (l, 0, j)),
                  pl.BlockSpec((None, 1, tn), lambda l, j: (l, 0, j))],
        out_specs=pl.BlockSpec((None, rows, tn), lambda l, j: (l, 0, j)),
        out_shape=jax.ShapeDtypeStruct((depth, rows, n_out), F32),
        name="adaln_mod",
    )(c_pad, w_ada, b_ada.reshape(depth, 1, n_out))
    return out[:, :c.shape[0]]


def _in_proj_kernel(x_ref, mod_ref, g_ref, w_ref, wkt_ref, *refs, widths, cm_groups):
    n_out = len(widths)
    col_refs = refs[:n_out]
    cm_refs = refs[n_out:n_out + len(cm_groups)]
    kt_ref = refs[n_out + len(cm_groups)]
    scr_refs = refs[n_out + len(cm_groups) + 1:]
    d_model = x_ref.shape[-1]
    x = x_ref[...]
    ms = jnp.mean(x * x, axis=-1, keepdims=True)
    shift = mod_ref[:, :d_model]
    scale = mod_ref[:, d_model:2 * d_model]
    h = (x * lax.rsqrt(ms + EPS)) * g_ref[...]
    h = (h * (1.0 + scale) + shift).astype(BF16)
    off = 0
    for gi, (ref, width) in enumerate(zip(col_refs, widths)):
        res = jnp.dot(h, w_ref[:, off:off + width], preferred_element_type=F32)
        ref[...] = res.astype(ref.dtype)
        if gi in cm_groups:
            cm_ref, scr = cm_refs[cm_groups.index(gi)], scr_refs[cm_groups.index(gi)]
            n_cls, per_cls = cm_ref.shape[0], cm_ref.shape[1]
            for k in range(width // LANES):
                lanes = slice(k * LANES, (k + 1) * LANES)
                scr[k] = res[:, lanes]
                for c in range(n_cls):
                    cm_ref[c, :, lanes] = scr[
                        k, pl.ds(_class_residue(c), per_cls, stride=n_cls), :].astype(cm_ref.dtype)
        off += width
    kt_ref[...] = lax.dot_general(wkt_ref[...], h, (((1,), (1,)), ((), ())),
                                  preferred_element_type=F32).astype(kt_ref.dtype)


def _in_proj(x, mod_l, g_pre_l, w_cols, w_kt, widths, cm_groups):
    bn, s, d_model = x.shape
    tm = TILE_M
    n_t = s // tm
    n_cols = w_cols.shape[1]
    kt_rows = w_kt.shape[0]
    per_cls = tm // N_CLASSES
    out_shape = [jax.ShapeDtypeStruct((bn, s, w), BF16) for w in widths]
    out_specs = [pl.BlockSpec((None, tm, w), lambda b, i: (b, i, 0)) for w in widths]
    for gi in cm_groups:
        out_shape.append(jax.ShapeDtypeStruct((bn, N_CLASSES, s // N_CLASSES, widths[gi]), BF16))
        out_specs.append(pl.BlockSpec((None, N_CLASSES, per_cls, widths[gi]),
                                      lambda b, i: (b, 0, i, 0)))
    out_shape.append(jax.ShapeDtypeStruct((bn, n_t, kt_rows, tm), BF16))
    out_specs.append(pl.BlockSpec((None, None, kt_rows, tm), lambda b, i: (b, i, 0, 0)))
    return pl.pallas_call(
        functools.partial(_in_proj_kernel, widths=widths, cm_groups=cm_groups),
        grid=(bn, n_t),
        in_specs=[pl.BlockSpec((None, tm, d_model), lambda b, i: (b, i, 0)),
                  pl.BlockSpec((None, 1, 3 * d_model), lambda b, i: (b, 0, 0)),
                  pl.BlockSpec((1, d_model), lambda b, i: (0, 0)),
                  pl.BlockSpec((d_model, n_cols), lambda b, i: (0, 0),
                               pipeline_mode=pl.Buffered(1)),
                  pl.BlockSpec((kt_rows, d_model), lambda b, i: (0, 0),
                               pipeline_mode=pl.Buffered(1))],
        out_specs=out_specs,
        out_shape=out_shape,
        scratch_shapes=[pltpu.VMEM((widths[gi] // LANES, tm, LANES), F32) for gi in cm_groups],
        compiler_params=pltpu.CompilerParams(
            dimension_semantics=("parallel", "parallel"), vmem_limit_bytes=VMEM_LIMIT),
        name="in_proj",
    )(x, mod_l.reshape(bn, 1, 3 * d_model), g_pre_l.reshape(1, d_model), w_cols, w_kt)


def _lane_masks():
    lane = lax.broadcasted_iota(jnp.int32, (1, LANES), 1)
    even = lane < HEAD_DIM
    m_even = jnp.where(even, 1.0, 0.0).astype(BF16)
    m_odd = jnp.where(even, 0.0, 1.0).astype(BF16)
    return even, m_even, m_odd


def _banded_kernel(*refs, n_pairs, pairs_per_kv, has_sink, pieces):
    it = iter(refs)
    sink_ref = next(it) if has_sink else None
    q_ref, kp_ref, kc_ref, vp_ref, vc_ref, bias_ref, o_ref = (next(it) for _ in range(7))
    lse_ref = None if has_sink else next(it)
    k_buf, v_buf, s_buf = next(it), next(it), next(it)
    step = pl.program_id(2)
    if pieces == 1:
        nblk = q_ref.shape[0] // BLK
        k_buf[:BLK] = kp_ref[...]
        k_buf[BLK:] = kc_ref[...]
        v_buf[:BLK] = vp_ref[...]
        v_buf[BLK:] = vc_ref[...]
        q_src, o_dst, lse_dst = q_ref, o_ref, lse_ref
    else:
        q_src, o_dst, lse_dst = next(it), next(it), next(it)
        per = BLK // pieces
        nblk = q_ref.shape[1] // per
        for pc in range(pieces):
            k_buf[pc * per:(pc + 1) * per] = kp_ref[pc]
            v_buf[pc * per:(pc + 1) * per] = vp_ref[pc]
            for ib in range(nblk):
                dst = (ib * pieces + pc) * per
                src = slice(ib * per, (ib + 1) * per)
                q_src[dst:dst + per] = q_ref[pc, src]
                k_buf[BLK + dst:BLK + dst + per] = kc_ref[pc, src]
                v_buf[BLK + dst:BLK + dst + per] = vc_ref[pc, src]
    even, m_even, m_odd = _lane_masks()
    lane = lax.broadcasted_iota(jnp.int32, (BLK, LANES), 1)
    top_rows = lax.broadcasted_iota(jnp.int32, (2 * BLK, 1), 0) < BLK
    ones = jnp.ones((2 * BLK, LANES), BF16)

    def block(ib, carry):
        r0 = pl.multiple_of(ib * BLK, BLK)
        variant = jnp.where((step == 0) & (ib == 0), 0, 1)
        lse_tile = jnp.ones((BLK, LANES), F32)
        for j in range(n_pairs):
            g = j // pairs_per_kv
            q2 = q_src[pl.ds(r0, BLK), j * LANES:(j + 1) * LANES]
            qs = jnp.concatenate([q2 * m_even, q2 * m_odd], axis=0)
            k2 = k_buf[pl.ds(r0, 2 * BLK), g * LANES:(g + 1) * LANES]
            s = lax.dot_general(qs, k2, (((1,), (1,)), ((), ())), preferred_element_type=F32)
            s_buf[j] = s + bias_ref[variant, j * 2 * BLK:(j + 1) * 2 * BLK, :]
        for j in range(n_pairs):
            g = j // pairs_per_kv
            v2 = v_buf[pl.ds(r0, 2 * BLK), g * LANES:(g + 1) * LANES]
            s = s_buf[j]
            m = jnp.max(s, axis=-1, keepdims=True)
            if has_sink:
                sk = jnp.where(top_rows, sink_ref[2 * j], sink_ref[2 * j + 1]) * LOG2E
                m = jnp.maximum(m, sk)
            e = jnp.exp2(s - m)
            ox = jnp.dot(e.astype(BF16), jnp.concatenate([v2, ones], axis=1),
                         preferred_element_type=F32)
            ov, den = ox[:, :LANES], ox[:, LANES:]
            if has_sink:
                ov = ov / (den + jnp.exp2(sk - m))
            o_dst[pl.ds(r0, BLK), j * LANES:(j + 1) * LANES] = (
                jnp.where(even, ov[:BLK], ov[BLK:]).astype(o_dst.dtype))
            if lse_dst is not None:
                for hh, half in ((2 * j, slice(0, BLK)), (2 * j + 1, slice(BLK, 2 * BLK))):
                    lse_tile = jnp.where(lane == hh, m[half], lse_tile)
                    lse_tile = jnp.where(lane == HEAD_DIM + hh, den[half], lse_tile)
        if lse_dst is not None:
            lse_dst[pl.ds(r0, BLK), :] = lse_tile
        return carry

    lax.fori_loop(0, nblk, block, 0)

    if pieces > 1:
        for pc in range(pieces):
            for ib in range(nblk):
                dst = (ib * pieces + pc) * per
                o_ref[pc, ib * per:(ib + 1) * per] = o_dst[dst:dst + per]
                lse_ref[pc, ib * per:(ib + 1) * per] = lse_dst[dst:dst + per]


def _banded_attention(q, k, v, bias, pairs_per_kv, layout, sinks=None):
    wq, wkv = q.shape[-1], k.shape[-1]
    bn = q.shape[0]
    n_pairs = wq // LANES
    has_sink = sinks is not None
    if layout == "natural":
        seq = q.shape[1]
        nblk = min(BAND_BLOCKS, seq // BLK)
        rows = nblk * BLK
        grid = (bn, 1, seq // rows)
        cur = lambda b, r, n: (b, n, 0)
        prev = lambda b, r, n: (b, jnp.maximum(n * nblk - 1, 0), 0)
        cur_blk = lambda w: (None, rows, w)
        prev_blk = lambda w: (None, BLK, w)
        pieces = 1
    elif layout == "classes":
        n_cls, seq = q.shape[1], q.shape[2]
        nblk = min(BAND_BLOCKS, seq // BLK)
        rows = nblk * BLK
        grid = (bn, n_cls, seq // rows)
        cur = lambda b, r, n: (b, r, n, 0)
        prev = lambda b, r, n: (b, r, jnp.maximum(n * nblk - 1, 0), 0)
        cur_blk = lambda w: (None, None, rows, w)
        prev_blk = lambda w: (None, None, BLK, w)
        pieces = 1
    else:
        n_cls, seq = q.shape[1], q.shape[2]
        pieces = BAND_PIECES
        per = BLK // pieces
        nblk = pieces
        rows = BLK
        grid = (bn, n_cls // pieces, seq // rows)
        cur = lambda b, r, n: (b, r, n, 0)
        prev = lambda b, r, n: (b, r, jnp.maximum(n * (rows // per) - 1, 0), 0)
        cur_blk = lambda w: (None, pieces, rows, w)
        prev_blk = lambda w: (None, pieces, per, w)
    tot = nblk * BLK
    in_specs = [pl.BlockSpec(cur_blk(wq), cur),
                pl.BlockSpec(prev_blk(wkv), prev),
                pl.BlockSpec(cur_blk(wkv), cur),
                pl.BlockSpec(prev_blk(wkv), prev),
                pl.BlockSpec(cur_blk(wkv), cur),
                pl.BlockSpec(bias.shape, lambda b, r, n: (0, 0, 0))]
    args = [q, k, k, v, v, bias]
    out_specs = [pl.BlockSpec(cur_blk(wq), cur)]
    out_shape = [jax.ShapeDtypeStruct(q.shape, BF16)]
    scratch = [pltpu.VMEM((tot + BLK, wkv), BF16), pltpu.VMEM((tot + BLK, wkv), BF16),
               pltpu.VMEM((n_pairs, 2 * BLK, 2 * BLK), F32)]
    if has_sink:
        in_specs.insert(0, pl.BlockSpec(memory_space=pltpu.SMEM))
        args.insert(0, sinks)
    else:
        out_specs.append(pl.BlockSpec(cur_blk(LANES), cur))
        out_shape.append(jax.ShapeDtypeStruct(q.shape[:-1] + (LANES,), F32))
    if pieces > 1:
        scratch += [pltpu.VMEM((tot, wq), BF16), pltpu.VMEM((tot, wq), BF16),
                    pltpu.VMEM((tot, LANES), F32)]
    outs = pl.pallas_call(
        functools.partial(_banded_kernel, n_pairs=n_pairs, pairs_per_kv=pairs_per_kv,
                          has_sink=has_sink, pieces=pieces),
        grid=grid,
        in_specs=in_specs,
        out_specs=out_specs,
        out_shape=out_shape,
        scratch_shapes=scratch,
        compiler_params=pltpu.CompilerParams(
            dimension_semantics=("parallel", "parallel", "arbitrary"),
            vmem_limit_bytes=VMEM_LIMIT),
        name="banded_sink" if has_sink else f"banded_{layout}",
    )(*args)
    return outs[0] if has_sink else tuple(outs)


def _diff_kernel(far_ref, q_ref, kt_ref, v_ref, bias_ref, lam_ref, g_ref, o_ref,
                 qs_s, vx_s, m_s, acc_s, s_s, *, lam_init, n_near):
    h = pl.program_id(1)
    i = pl.program_id(2)
    t = q_ref.shape[0]
    n_rep = t // LANES

    @pl.when(i == 0)
    def _():
        vx_s[:, :LANES] = v_ref[...]
        vx_s[:, LANES:] = jnp.ones((vx_s.shape[0], LANES), BF16)

    _, m_even, m_odd = _lane_masks()
    q2 = q_ref[...]
    qs_s[:t] = q2 * m_even
    qs_s[t:] = q2 * m_odd
    m_s[...] = jnp.full(m_s.shape, NEG, F32)
    acc_s[...] = jnp.zeros(acc_s.shape, F32)
    far_bias = far_ref[h] * LOG2E

    def sweep(j0, n_tiles, near):
        k0 = pl.multiple_of(j0 * t, t)
        kts = [kt_ref[j0 + a] for a in range(n_tiles)]
        vx = vx_s[pl.ds(k0, n_tiles * t), :]
        width = n_tiles * t
        for r0 in range(0, 2 * t, DIFF_ROWS):
            rows = pl.ds(r0, DIFF_ROWS)
            qc = qs_s[rows, :]
            for a, kt in enumerate(kts):
                part = jnp.dot(qc, kt, preferred_element_type=F32)
                if near:
                    part = part + bias_ref[jnp.minimum(i - j0 - a, n_near),
                                           pl.ds(r0 % t, DIFF_ROWS), :]
                s_s[rows, a * t:(a + 1) * t] = part
        shift = 0.0 if near else far_bias
        for r0 in range(0, 2 * t, DIFF_ROWS):
            rows = pl.ds(r0, DIFF_ROWS)
            s = s_s[rows, :width]
            m_cur = jnp.max(s, axis=-1, keepdims=True) + shift
            m_old = m_s[rows, :]
            m_new = jnp.maximum(m_old, m_cur)
            alpha = jnp.exp2(m_old - m_new)
            m_sub = m_new - shift
            p = jnp.exp2(s - jnp.concatenate([m_sub] * (n_tiles * n_rep), axis=1))
            pv = jnp.dot(p.astype(BF16), vx, preferred_element_type=F32)
            acc_s[rows, :] = jnp.concatenate([alpha, alpha], axis=1) * acc_s[rows, :] + pv
            m_s[rows, :] = m_new

    n_far = jnp.maximum(i - (n_near - 1), 0)
    n_quads = n_far // DIFF_MAX_TILES
    j_pairs = DIFF_MAX_TILES * n_quads
    n_far_pairs = (n_far - j_pairs) >> 1
    j_near = j_pairs + 2 * n_far_pairs
    odd = (i + 1 - j_near) & 1

    def far_quad(jj, carry):
        sweep(DIFF_MAX_TILES * jj, DIFF_MAX_TILES, False)
        return carry

    def far_pair(jj, carry):
        sweep(j_pairs + 2 * jj, 2, False)
        return carry

    lax.fori_loop(0, n_quads, far_quad, 0)
    lax.fori_loop(0, n_far_pairs, far_pair, 0)

    @pl.when(odd == 1)
    def _():
        sweep(j_near, 1, True)

    def near_pair(kk, carry):
        sweep(j_near + odd + 2 * kk, 2, True)
        return carry

    lax.fori_loop(0, (i + 1 - j_near) >> 1, near_pair, 0)

    acc = acc_s[...]
    o = acc[:, :LANES] / acc[:, LANES:]
    lam = (jnp.exp(jnp.sum(lam_ref[0:1, :] * lam_ref[1:2, :], axis=-1, keepdims=True))
           - jnp.exp(jnp.sum(lam_ref[2:3, :] * lam_ref[3:4, :], axis=-1, keepdims=True)) + lam_init)
    y = o[:t] - lam * o[t:]
    y = y * lax.rsqrt(jnp.mean(y * y, axis=-1, keepdims=True) + EPS)
    o_ref[...] = (y * g_ref[...] * (1.0 - lam_init)).astype(o_ref.dtype)


def _diff_attention(q, kt, v, bias_c, far_bias, lam_params, g_sub_l, lam_init):
    bn, s, w = q.shape
    t = TILE_M
    n_t = s // t
    n_heads = w // LANES
    n_near = bias_c.shape[1] - 1
    return pl.pallas_call(
        functools.partial(_diff_kernel, lam_init=lam_init, n_near=n_near),
        grid=(bn, n_heads, n_t),
        in_specs=[pl.BlockSpec(memory_space=pltpu.SMEM),
                  pl.BlockSpec((None, t, LANES), lambda b, h, i: (b, i, h)),
                  pl.BlockSpec((None, n_t, LANES, t), lambda b, h, i: (b, 0, h, 0)),
                  pl.BlockSpec((None, s, LANES), lambda b, h, i: (b, 0, h)),
                  pl.BlockSpec((None, n_near + 1, t, t), lambda b, h, i: (h, 0, 0, 0)),
                  pl.BlockSpec((4, HEAD_DIM), lambda b, h, i: (0, 0)),
                  pl.BlockSpec((1, LANES), lambda b, h, i: (0, 0))],
        out_specs=pl.BlockSpec((None, t, LANES), lambda b, h, i: (b, i, h)),
        out_shape=jax.ShapeDtypeStruct((bn, s, w), BF16),
        scratch_shapes=[pltpu.VMEM((2 * t, LANES), BF16),
                        pltpu.VMEM((s, 2 * LANES), BF16),
                        pltpu.VMEM((2 * t, LANES), F32),
                        pltpu.VMEM((2 * t, 2 * LANES), F32),
                        pltpu.VMEM((2 * t, DIFF_MAX_TILES * t), F32)],
        compiler_params=pltpu.CompilerParams(
            dimension_semantics=("parallel", "parallel", "arbitrary"),
            vmem_limit_bytes=VMEM_LIMIT),
        name="diff_attn",
    )(far_bias, q, kt, v, bias_c, lam_params, g_sub_l.reshape(1, LANES))


def _out_proj_kernel(x_ref, mod_ref, g_ref, w_ref, e_ref, ya_ref, o1_ref, o4_ref, o16_ref,
                     l1_ref, l4_ref, l16_ref, yc_ref, z_ref, out_ref,
                     o4_scr, o16_scr, l4_scr, l16_scr):
    d_model = x_ref.shape[-1]
    wa = ya_ref.shape[-1]
    wb = o1_ref.shape[-1]
    z = z_ref[...].astype(F32)
    sz = z / (1.0 + jnp.exp(-z))

    def to_natural(cm_ref, scr):
        n_cls, per_cls = cm_ref.shape[0], cm_ref.shape[1]
        for k in range(scr.shape[0]):
            for c in range(n_cls):
                scr[k, pl.ds(_class_residue(c), per_cls, stride=n_cls), :] = cm_ref[
                    c, :, k * LANES:(k + 1) * LANES].astype(F32)
        return jnp.concatenate([scr[k] for k in range(scr.shape[0])], axis=1)

    o_parts = [o1_ref[...].astype(F32), to_natural(o4_ref, o4_scr), to_natural(o16_ref, o16_scr)]
    stats = [l1_ref[...], to_natural(l4_ref, l4_scr), to_natural(l16_ref, l16_scr)]
    head_lane = lax.broadcasted_iota(jnp.int32, stats[0].shape, 1) < wb // HEAD_DIM
    maxes = [jnp.where(head_lane, st, 0.0) for st in stats]
    dens = [jnp.where(head_lane, pltpu.roll(st, HEAD_DIM, 1), 1.0) for st in stats]
    lses = [mxp + jnp.log2(dn) for mxp, dn in zip(maxes, dens)]
    mx = jnp.maximum(jnp.maximum(lses[0], lses[1]), lses[2])
    ws = [jnp.exp2(l - mx) for l in lses]
    tot = ws[0] + ws[1] + ws[2]
    yb = jnp.zeros((x_ref.shape[0], wb), F32)
    for wgt, dn, o_part in zip(ws, dens, o_parts):
        a_hi, a_lo = _split_bf16(wgt / (tot * dn))
        spread = (jnp.dot(a_hi, e_ref[...], preferred_element_type=F32)
                  + jnp.dot(a_lo, e_ref[...], preferred_element_type=F32))
        yb = yb + spread * o_part

    ga = (ya_ref[...].astype(F32) * sz[:, :wa]).astype(BF16)
    gb = (yb * sz[:, wa:wa + wb]).astype(BF16)
    gc = (yc_ref[...].astype(F32) * sz[:, wa + wb:]).astype(BF16)
    y = jnp.dot(ga, w_ref[:wa], preferred_element_type=F32)
    y += jnp.dot(gb, w_ref[wa:wa + wb], preferred_element_type=F32)
    y += jnp.dot(gc, w_ref[wa + wb:], preferred_element_type=F32)
    r = (y * lax.rsqrt(jnp.mean(y * y, axis=-1, keepdims=True) + EPS)) * g_ref[...]
    gate = mod_ref[:, 2 * d_model:]
    out_ref[...] = x_ref[...] + gate * r


def _out_proj(x, mod_l, g_post_l, w_out_l, spread, ya, ob, lb, yc, z):
    bn, s, d_model = x.shape
    tm = TILE_M
    per_cls = tm // N_CLASSES
    wb = ob[0].shape[-1]
    row = lambda w: pl.BlockSpec((None, tm, w), lambda b, i: (b, i, 0))
    cm = lambda w: pl.BlockSpec((None, N_CLASSES, per_cls, w), lambda b, i: (b, 0, i, 0))
    whole = lambda a: pl.BlockSpec(a.shape, lambda b, i: (0,) * a.ndim,
                                   pipeline_mode=pl.Buffered(1))
    g2 = g_post_l.reshape(1, d_model)
    return pl.pallas_call(
        _out_proj_kernel,
        grid=(bn, s // tm),
        in_specs=[row(d_model),
                  pl.BlockSpec((None, 1, 3 * d_model), lambda b, i: (b, 0, 0)),
                  whole(g2), whole(w_out_l), whole(spread),
                  row(ya.shape[-1]),
                  row(wb), cm(wb), cm(wb),
                  row(LANES), cm(LANES), cm(LANES),
                  row(yc.shape[-1]), row(z.shape[-1])],
        out_specs=row(d_model),
        out_shape=jax.ShapeDtypeStruct(x.shape, x.dtype),
        scratch_shapes=[pltpu.VMEM((wb // LANES, tm, LANES), F32),
                        pltpu.VMEM((wb // LANES, tm, LANES), F32),
                        pltpu.VMEM((1, tm, LANES), F32), pltpu.VMEM((1, tm, LANES), F32)],
        compiler_params=pltpu.CompilerParams(
            dimension_semantics=("parallel", "parallel"), vmem_limit_bytes=VMEM_LIMIT),
        name="out_proj",
    )(x, mod_l.reshape(bn, 1, 3 * d_model), g2, w_out_l, spread, ya, *ob, *lb, yc, z)


def _prepare_w_in(w, sizes, n_kv_a):
    offs = np.concatenate([[0], np.cumsum(sizes)])
    grp = lambda i: w[:, int(offs[i]):int(offs[i + 1])]
    aq, ak, av, bq, bk, bv, cq, ck, cv, z = [grp(i) for i in range(10)]

    def dup(m):
        parts = []
        for g in range(n_kv_a):
            head = m[:, g * HEAD_DIM:(g + 1) * HEAD_DIM]
            parts += [head, head]
        return jnp.concatenate(parts, axis=1)

    groups = [aq * Q_SCALE, dup(ak), dup(av), bq * Q_SCALE, bk, bv, cq * Q_SCALE, cv, z]
    widths = tuple(int(g.shape[1]) for g in groups)
    w_cols = jnp.concatenate(groups, axis=1).astype(BF16)
    return w_cols, ck.T.astype(BF16), widths


def kernel(x, c, rel_table, w_in, w_out, w_ada, b_ada, g_pre, g_post, a_sinks,
           lam_q1, lam_k1, lam_q2, lam_k2, g_sub):
    bn, s, d_model = x.shape
    depth = w_in.shape[0]
    d_mix = w_out.shape[1]
    a_width = b_width = 3 * d_mix // 8
    c_width = d_mix // 4
    n_a_heads = a_width // HEAD_DIM
    n_kv_a = n_a_heads // 4
    n_b_heads = b_width // HEAD_DIM
    n_c_heads = c_width // (2 * HEAD_DIM)
    sizes = (a_width, n_kv_a * HEAD_DIM, n_kv_a * HEAD_DIM, b_width, b_width, b_width,
             c_width, c_width, c_width, d_mix)
    assert w_in.shape[2] == sum(sizes) and rel_table.shape == (REL_BUCKETS,
                                                                n_a_heads + n_b_heads + n_c_heads)
    assert n_a_heads == n_b_heads and s % (N_CLASSES * BLK) == 0
    assert [d for _, d in B_PATTERNS] == [1, BAND_PIECES, N_CLASSES]
    assert s % TILE_M == 0 and TILE_M % BLK == 0

    n_near = -(-(FAR_DIST - 1) // TILE_M) + 1
    bias_a, bias_b, bias_c = _bias_tables(rel_table, n_a_heads, n_b_heads, n_c_heads,
                                          TILE_M, n_near + 1)
    far_bias = rel_table[REL_BUCKETS - 1, n_a_heads + n_b_heads:]
    mod = _modulation(c, w_ada, b_ada)

    head_of_lane = np.arange(b_width) // HEAD_DIM
    spread = jnp.asarray(np.arange(LANES)[:, None] == head_of_lane[None, :], BF16)

    for l in range(depth):
        w_cols, w_kt, widths = _prepare_w_in(w_in[l], sizes, n_kv_a)
        (aq, ak, av, bq, bk, bv, cq, cv, z, bq_cm, bk_cm, bv_cm, ckt) = _in_proj(
            x, mod[l], g_pre[l], w_cols, w_kt, widths, cm_groups=(3, 4, 5))

        ya = _banded_attention(aq, ak, av, bias_a, 2, "natural", sinks=a_sinks[l])
        o1, l1 = _banded_attention(bq, bk, bv, bias_b[0], 1, "natural")
        o4, l4 = _banded_attention(bq_cm, bk_cm, bv_cm, bias_b[1], 1, "pieces")
        o16, l16 = _banded_attention(bq_cm, bk_cm, bv_cm, bias_b[2], 1, "classes")
        ob, lb = (o1, o4, o16), (l1, l4, l16)

        lam_init = 0.8 - 0.6 * math.exp(-0.3 * l)
        lam_params = jnp.stack([lam_q1[l], lam_k1[l], lam_q2[l], lam_k2[l]]).astype(F32)
        yc = _diff_attention(cq, ckt, cv, bias_c, far_bias, lam_params, g_sub[l], lam_init)

        x = _out_proj(x, mod[l], g_post[l], w_out[l].astype(BF16), spread, ya, ob, lb, yc, z)
    return x
```

```python
import functools
import math

import numpy as np
import jax
import jax.numpy as jnp
from jax import lax
from jax.experimental import pallas as pl
from jax.experimental.pallas import tpu as pltpu

F32 = jnp.float32
BF16 = jnp.bfloat16

HEAD_DIM = 64
LANES = 128
BLK = 128
A_WINDOW = 128
B_PATTERNS = ((128, 1), (512, 4), (2048, 16))
REL_BUCKETS = 32
REL_MAX_DIST = 2048
EPS = 1e-6
NEG = -1e30
LOG2E = 1.4426950408889634
Q_SCALE = LOG2E / math.sqrt(HEAD_DIM)

TILE_M = 512
BAND_BLOCKS = 8
N_CLASSES = 16
BAND_PIECES = 4
PIECED_PATTERN = 1
DIFF_ROWS = 128
DIFF_MAX_TILES = 4
VMEM_LIMIT = 56 * 1024 * 1024


def _class_residue(c):
    return c // BAND_PIECES + (N_CLASSES // BAND_PIECES) * (c % BAND_PIECES)


def _rel_bucket_np(n):
    n = np.maximum(np.asarray(n, np.int64), 0)
    max_exact = REL_BUCKETS // 2
    nf = np.maximum(n, 1).astype(np.float32)
    large = max_exact + (np.log(nf / np.float32(max_exact))
                         / np.float32(math.log(REL_MAX_DIST / max_exact))
                         * np.float32(REL_BUCKETS - max_exact)).astype(np.int32)
    large = np.minimum(large, REL_BUCKETS - 1)
    return np.where(n < max_exact, n, large)


_BUCKET_OF = _rel_bucket_np(np.arange(0, 1 << 18))
_THR = [0] + [int(np.argmax(_BUCKET_OF >= b)) for b in range(1, REL_BUCKETS)]
FAR_DIST = _THR[REL_BUCKETS - 1]


def _bucket_values(dist, dist_scale, lo, hi, table_ref, col):
    b_lo = int(_BUCKET_OF[max(lo, 0) * dist_scale])
    b_hi = int(_BUCKET_OF[max(hi, 0) * dist_scale])
    val = jnp.full(dist.shape, table_ref[b_lo, col] * LOG2E, F32)
    for b in range(b_lo + 1, b_hi + 1):
        thr = -(-_THR[b] // dist_scale)
        val = jnp.where(dist >= thr, table_ref[b, col] * LOG2E, val)
    return val


def _band_bias_kernel(table_ref, a_ref, b_ref):
    h = pl.program_id(0)
    n_a_heads = pl.num_programs(0)
    row = lax.broadcasted_iota(jnp.int32, (BLK, 2 * BLK), 0)
    col = lax.broadcasted_iota(jnp.int32, (BLK, 2 * BLK), 1)
    first_ok = col >= BLK

    def pieced(idx):
        per = BLK // BAND_PIECES
        return (idx & (per - 1)) * BAND_PIECES + (idx >> (per.bit_length() - 1))

    def tile(dist, max_dist, dist_scale, table_col):
        val = _bucket_values(dist, dist_scale, 0, max_dist, table_ref, table_col)
        val = jnp.where((dist >= 0) & (dist <= max_dist), val, NEG)
        return jnp.where(first_ok, val, NEG), val

    dist = row + BLK - col
    a_ref[0], a_ref[1] = tile(dist, A_WINDOW - 1, 1, h)
    for p, (w, d) in enumerate(B_PATTERNS):
        if p == PIECED_PATTERN:
            within = col & (BLK - 1)
            dist_p = pieced(row) + BLK - (pieced(within) + (col - within))
        else:
            dist_p = dist
        b_ref[p, 0], b_ref[p, 1] = tile(dist_p, w // d, d, n_a_heads + h)


def _diff_bias_kernel(table_ref, out_ref, *, tile, n_tiles, head_offset):
    h = pl.program_id(0)
    row = lax.broadcasted_iota(jnp.int32, (BLK, BLK), 0)
    col = lax.broadcasted_iota(jnp.int32, (BLK, BLK), 1)
    sub = tile // BLK
    pieces = {}
    for delta in range(-(sub - 1), n_tiles * sub):
        if delta < 0:
            pieces[delta] = jnp.full((BLK, BLK), NEG, F32)
            continue
        dist = row - col + delta * BLK
        val = _bucket_values(dist, 1, delta * BLK - (BLK - 1), delta * BLK + BLK - 1,
                             table_ref, head_offset + h)
        pieces[delta] = jnp.where(dist >= 0, val, NEG) if delta == 0 else val
    for dt in range(n_tiles):
        for a in range(sub):
            for b in range(sub):
                out_ref[dt, a * BLK:(a + 1) * BLK, b * BLK:(b + 1) * BLK] = pieces[dt * sub + a - b]


def _bias_tables(rel_table, n_a_heads, n_b_heads, n_c_heads, tile, n_tiles):
    smem = pl.BlockSpec(memory_space=pltpu.SMEM)
    bias_a, bias_b = pl.pallas_call(
        _band_bias_kernel,
        grid=(n_a_heads,),
        in_specs=[smem],
        out_specs=[pl.BlockSpec((2, BLK, 2 * BLK), lambda h: (0, h, 0)),
                   pl.BlockSpec((3, 2, BLK, 2 * BLK), lambda h: (0, 0, h, 0))],
        out_shape=[jax.ShapeDtypeStruct((2, n_a_heads * BLK, 2 * BLK), F32),
                   jax.ShapeDtypeStruct((3, 2, n_b_heads * BLK, 2 * BLK), F32)],
        name="band_bias",
    )(rel_table)
    bias_c = pl.pallas_call(
        functools.partial(_diff_bias_kernel, tile=tile, n_tiles=n_tiles,
                          head_offset=n_a_heads + n_b_heads),
        grid=(n_c_heads,),
        in_specs=[smem],
        out_specs=pl.BlockSpec((None, n_tiles, tile, tile), lambda h: (h, 0, 0, 0)),
        out_shape=jax.ShapeDtypeStruct((n_c_heads, n_tiles, tile, tile), F32),
        compiler_params=pltpu.CompilerParams(vmem_limit_bytes=VMEM_LIMIT),
        name="diff_bias",
    )(rel_table)
    return bias_a, bias_b, bias_c


def _split_bf16(v):
    hi = v.astype(BF16)
    lo = (v - hi.astype(F32)).astype(BF16)
    return hi, lo


def _mod_kernel(c_ref, w_ref, b_ref, out_ref):
    c = c_ref[...]
    s = c / (1.0 + jnp.exp(-c))
    s_hi, s_lo = _split_bf16(s)
    w_hi, w_lo = _split_bf16(w_ref[...])
    acc = jnp.dot(s_hi, w_hi, preferred_element_type=F32)
    acc += jnp.dot(s_lo, w_hi, preferred_element_type=F32)
    acc += jnp.dot(s_hi, w_lo, preferred_element_type=F32)
    out_ref[...] = acc + b_ref[...]


def _modulation(c, w_ada, b_ada):
    depth, d_model, n_out = w_ada.shape
    rows = 8
    c_pad = jnp.zeros((rows, d_model), F32).at[:c.shape[0]].set(c)
    tn = 768
    out = pl.pallas_call(
        _mod_kernel,
        grid=(depth, n_out // tn),
        in_specs=[pl.BlockSpec((rows, d_model), lambda l, j: (0, 0)),
                  pl.BlockSpec((None, d_model, tn), lambda l, j: (l, 0, j)),
                  pl.BlockSpec((None, 1, tn), lambda l, j: (l, 0, j))],
        out_specs=pl.BlockSpec((None, rows, tn), lambda l, j: (l, 0, j)),
        out_shape=jax.ShapeDtypeStruct((depth, rows, n_out), F32),
        name="adaln_mod",
    )(c_pad, w_ada, b_ada.reshape(depth, 1, n_out))
    return out[:, :c.shape[0]]


def _in_proj_kernel(x_ref, mod_ref, g_ref, w_ref, wkt_ref, *refs, widths, cm_groups):
    n_out = len(widths)
    col_refs = refs[:n_out]
    cm_refs = refs[n_out:n_out + len(cm_groups)]
    kt_ref = refs[n_out + len(cm_groups)]
    scr_refs = refs[n_out + len(cm_groups) + 1:]
    d_model = x_ref.shape[-1]
    x = x_ref[...]
    ms = jnp.mean(x * x, axis=-1, keepdims=True)
    shift = mod_ref[:, :d_model]
    scale = mod_ref[:, d_model:2 * d_model]
    h = (x * lax.rsqrt(ms + EPS)) * g_ref[...]
    h = (h * (1.0 + scale) + shift).astype(BF16)
    off = 0
    for gi, (ref, width) in enumerate(zip(col_refs, widths)):
        res = jnp.dot(h, w_ref[:, off:off + width], preferred_element_type=F32)
        ref[...] = res.astype(ref.dtype)
        if gi in cm_groups:
            cm_ref, scr = cm_refs[cm_groups.index(gi)], scr_refs[cm_groups.index(gi)]
            n_cls, per_cls = cm_ref.shape[0], cm_ref.shape[1]
            for k in range(width // LANES):
                lanes = slice(k * LANES, (k + 1) * LANES)
                scr[k] = res[:, lanes]
                for c in range(n_cls):
                    cm_ref[c, :, lanes] = scr[
                        k, pl.ds(_class_residue(c), per_cls, stride=n_cls), :].astype(cm_ref.dtype)
        off += width
    kt_ref[...] = lax.dot_general(wkt_ref[...], h, (((1,), (1,)), ((), ())),
                                  preferred_element_type=F32).astype(kt_ref.dtype)


def _in_proj(x, mod_l, g_pre_l, w_cols, w_kt, widths, cm_groups):
    bn, s, d_model = x.shape
    tm = TILE_M
    n_t = s // tm
    n_cols = w_cols.shape[1]
    kt_rows = w_kt.shape[0]
    per_cls = tm // N_CLASSES
    out_shape = [jax.ShapeDtypeStruct((bn, s, w), BF16) for w in widths]
    out_specs = [pl.BlockSpec((None, tm, w), lambda b, i: (b, i, 0)) for w in widths]
    for gi in cm_groups:
        out_shape.append(jax.ShapeDtypeStruct((bn, N_CLASSES, s // N_CLASSES, widths[gi]), BF16))
        out_specs.append(pl.BlockSpec((None, N_CLASSES, per_cls, widths[gi]),
                                      lambda b, i: (b, 0, i, 0)))
    out_shape.append(jax.ShapeDtypeStruct((bn, n_t, kt_rows, tm), BF16))
    out_specs.append(pl.BlockSpec((None, None, kt_rows, tm), lambda b, i: (b, i, 0, 0)))
    return pl.pallas_call(
        functools.partial(_in_proj_kernel, widths=widths, cm_groups=cm_groups),
        grid=(bn, n_t),
        in_specs=[pl.BlockSpec((None, tm, d_model), lambda b, i: (b, i, 0)),
                  pl.BlockSpec((None, 1, 3 * d_model), lambda b, i: (b, 0, 0)),
                  pl.BlockSpec((1, d_model), lambda b, i: (0, 0)),
                  pl.BlockSpec((d_model, n_cols), lambda b, i: (0, 0),
                               pipeline_mode=pl.Buffered(1)),
                  pl.BlockSpec((kt_rows, d_model), lambda b, i: (0, 0),
                               pipeline_mode=pl.Buffered(1))],
        out_specs=out_specs,
        out_shape=out_shape,
        scratch_shapes=[pltpu.VMEM((widths[gi] // LANES, tm, LANES), F32) for gi in cm_groups],
        compiler_params=pltpu.CompilerParams(
            dimension_semantics=("parallel", "parallel"), vmem_limit_bytes=VMEM_LIMIT),
        name="in_proj",
    )(x, mod_l.reshape(bn, 1, 3 * d_model), g_pre_l.reshape(1, d_model), w_cols, w_kt)


def _lane_masks():
    lane = lax.broadcasted_iota(jnp.int32, (1, LANES), 1)
    even = lane < HEAD_DIM
    m_even = jnp.where(even, 1.0, 0.0).astype(BF16)
    m_odd = jnp.where(even, 0.0, 1.0).astype(BF16)
    return even, m_even, m_odd


def _banded_kernel(*refs, n_pairs, pairs_per_kv, has_sink, pieces):
    it = iter(refs)
    sink_ref = next(it) if has_sink else None
    q_ref, kp_ref, kc_ref, vp_ref, vc_ref, bias_ref, o_ref = (next(it) for _ in range(7))
    lse_ref = None if has_sink else next(it)
    k_buf, v_buf, s_buf = next(it), next(it), next(it)
    step = pl.program_id(2)
    if pieces == 1:
        nblk = q_ref.shape[0] // BLK
        k_buf[:BLK] = kp_ref[...]
        k_buf[BLK:] = kc_ref[...]
        v_buf[:BLK] = vp_ref[...]
        v_buf[BLK:] = vc_ref[...]
        q_src, o_dst, lse_dst = q_ref, o_ref, lse_ref
    else:
        q_src, o_dst, lse_dst = next(it), next(it), next(it)
        per = BLK // pieces
        nblk = q_ref.shape[1] // per
        for pc in range(pieces):
            k_buf[pc * per:(pc + 1) * per] = kp_ref[pc]
            v_buf[pc * per:(pc + 1) * per] = vp_ref[pc]
            for ib in range(nblk):
                dst = (ib * pieces + pc) * per
                src = slice(ib * per, (ib + 1) * per)
                q_src[dst:dst + per] = q_ref[pc, src]
                k_buf[BLK + dst:BLK + dst + per] = kc_ref[pc, src]
                v_buf[BLK + dst:BLK + dst + per] = vc_ref[pc, src]
    even, m_even, m_odd = _lane_masks()
    lane = lax.broadcasted_iota(jnp.int32, (BLK, LANES), 1)
    top_rows = lax.broadcasted_iota(jnp.int32, (2 * BLK, 1), 0) < BLK
    ones = jnp.ones((2 * BLK, LANES), BF16)

    for ib in range(nblk):
        r0 = ib * BLK
        variant = jnp.where(step == 0, 0, 1) if ib == 0 else 1
        lse_tile = jnp.ones((BLK, LANES), F32)
        for j in range(n_pairs):
            g = j // pairs_per_kv
            q2 = q_src[pl.ds(r0, BLK), j * LANES:(j + 1) * LANES]
            qs = jnp.concatenate([q2 * m_even, q2 * m_odd], axis=0)
            k2 = k_buf[pl.ds(r0, 2 * BLK), g * LANES:(g + 1) * LANES]
            s = lax.dot_general(qs, k2, (((1,), (1,)), ((), ())), preferred_element_type=F32)
            s_buf[ib * n_pairs + j] = s + bias_ref[variant, j * 2 * BLK:(j + 1) * 2 * BLK, :]
        for j in range(n_pairs):
            g = j // pairs_per_kv
            v2 = v_buf[pl.ds(r0, 2 * BLK), g * LANES:(g + 1) * LANES]
            s = s_buf[ib * n_pairs + j]
            m = jnp.max(s, axis=-1, keepdims=True)
            if has_sink:
                sk = jnp.where(top_rows, sink_ref[2 * j], sink_ref[2 * j + 1]) * LOG2E
                m = jnp.maximum(m, sk)
            e = jnp.exp2(s - m)
            ox = jnp.dot(e.astype(BF16), jnp.concatenate([v2, ones], axis=1),
                         preferred_element_type=F32)
            ov, den = ox[:, :LANES], ox[:, LANES:]
            if has_sink:
                ov = ov / (den + jnp.exp2(sk - m))
            o_dst[pl.ds(r0, BLK), j * LANES:(j + 1) * LANES] = (
                jnp.where(even, ov[:BLK], ov[BLK:]).astype(o_dst.dtype))
            if lse_dst is not None:
                for hh, half in ((2 * j, slice(0, BLK)), (2 * j + 1, slice(BLK, 2 * BLK))):
                    lse_tile = jnp.where(lane == hh, m[half], lse_tile)
                    lse_tile = jnp.where(lane == HEAD_DIM + hh, den[half], lse_tile)
        if lse_dst is not None:
            lse_dst[pl.ds(r0, BLK), :] = lse_tile

    if pieces > 1:
        for pc in range(pieces):
            for ib in range(nblk):
                dst = (ib * pieces + pc) * per
                o_ref[pc, ib * per:(ib + 1) * per] = o_dst[dst:dst + per]
                lse_ref[pc, ib * per:(ib + 1) * per] = lse_dst[dst:dst + per]


def _banded_attention(q, k, v, bias, pairs_per_kv, layout, sinks=None):
    wq, wkv = q.shape[-1], k.shape[-1]
    bn = q.shape[0]
    n_pairs = wq // LANES
    has_sink = sinks is not None
    if layout == "natural":
        seq = q.shape[1]
        nblk = min(BAND_BLOCKS, seq // BLK)
        rows = nblk * BLK
        grid = (bn, 1, seq // rows)
        cur = lambda b, r, n: (b, n, 0)
        prev = lambda b, r, n: (b, jnp.maximum(n * nblk - 1, 0), 0)
        cur_blk = lambda w: (None, rows, w)
        prev_blk = lambda w: (None, BLK, w)
        pieces = 1
    elif layout == "classes":
        n_cls, seq = q.shape[1], q.shape[2]
        nblk = min(BAND_BLOCKS, seq // BLK)
        rows = nblk * BLK
        grid = (bn, n_cls, seq // rows)
        cur = lambda b, r, n: (b, r, n, 0)
        prev = lambda b, r, n: (b, r, jnp.maximum(n * nblk - 1, 0), 0)
        cur_blk = lambda w: (None, None, rows, w)
        prev_blk = lambda w: (None, None, BLK, w)
        pieces = 1
    else:
        n_cls, seq = q.shape[1], q.shape[2]
        pieces = BAND_PIECES
        per = BLK // pieces
        nblk = pieces
        rows = BLK
        grid = (bn, n_cls // pieces, seq // rows)
        cur = lambda b, r, n: (b, r, n, 0)
        prev = lambda b, r, n: (b, r, jnp.maximum(n * (rows // per) - 1, 0), 0)
        cur_blk = lambda w: (None, pieces, rows, w)
        prev_blk = lambda w: (None, pieces, per, w)
    tot = nblk * BLK
    in_specs = [pl.BlockSpec(cur_blk(wq), cur),
                pl.BlockSpec(prev_blk(wkv), prev),
                pl.BlockSpec(cur_blk(wkv), cur),
                pl.BlockSpec(prev_blk(wkv), prev),
                pl.BlockSpec(cur_blk(wkv), cur),
                pl.BlockSpec(bias.shape, lambda b, r, n: (0, 0, 0))]
    args = [q, k, k, v, v, bias]
    out_specs = [pl.BlockSpec(cur_blk(wq), cur)]
    out_shape = [jax.ShapeDtypeStruct(q.shape, BF16)]
    scratch = [pltpu.VMEM((tot + BLK, wkv), BF16), pltpu.VMEM((tot + BLK, wkv), BF16),
               pltpu.VMEM((nblk * n_pairs, 2 * BLK, 2 * BLK), F32)]
    if has_sink:
        in_specs.insert(0, pl.BlockSpec(memory_space=pltpu.SMEM))
        args.insert(0, sinks)
    else:
        out_specs.append(pl.BlockSpec(cur_blk(LANES), cur))
        out_shape.append(jax.ShapeDtypeStruct(q.shape[:-1] + (LANES,), F32))
    if pieces > 1:
        scratch += [pltpu.VMEM((tot, wq), BF16), pltpu.VMEM((tot, wq), BF16),
                    pltpu.VMEM((tot, LANES), F32)]
    outs = pl.pallas_call(
        functools.partial(_banded_kernel, n_pairs=n_pairs, pairs_per_kv=pairs_per_kv,
                          has_sink=has_sink, pieces=pieces),
        grid=grid,
        in_specs=in_specs,
        out_specs=out_specs,
        out_shape=out_shape,
        scratch_shapes=scratch,
        compiler_params=pltpu.CompilerParams(
            dimension_semantics=("parallel", "parallel", "arbitrary"),
            vmem_limit_bytes=VMEM_LIMIT),
        name="banded_sink" if has_sink else f"banded_{layout}",
    )(*args)
    return outs[0] if has_sink else tuple(outs)


def _diff_kernel(far_ref, q_ref, kt_ref, v_ref, bias_ref, lam_ref, g_ref, o_ref,
                 qs_s, vx_s, m_s, acc_s, s_s, *, lam_init, n_near):
    h = pl.program_id(1)
    i = pl.program_id(2)
    t = q_ref.shape[0]
    n_rep = t // LANES

    @pl.when(i == 0)
    def _():
        vx_s[:, :LANES] = v_ref[...]
        vx_s[:, LANES:] = jnp.ones((vx_s.shape[0], LANES), BF16)

    _, m_even, m_odd = _lane_masks()
    q2 = q_ref[...]
    qs_s[:t] = q2 * m_even
    qs_s[t:] = q2 * m_odd
    m_s[...] = jnp.full(m_s.shape, NEG, F32)
    acc_s[...] = jnp.zeros(acc_s.shape, F32)
    far_bias = far_ref[h] * LOG2E

    def sweep(j0, n_tiles, near):
        k0 = pl.multiple_of(j0 * t, t)
        kts = [kt_ref[j0 + a] for a in range(n_tiles)]
        vx = vx_s[pl.ds(k0, n_tiles * t), :]
        width = n_tiles * t
        for r0 in range(0, 2 * t, DIFF_ROWS):
            rows = pl.ds(r0, DIFF_ROWS)
            qc = qs_s[rows, :]
            for a, kt in enumerate(kts):
                part = jnp.dot(qc, kt, preferred_element_type=F32)
                if near:
                    part = part + bias_ref[jnp.minimum(i - j0 - a, n_near),
                                           pl.ds(r0 % t, DIFF_ROWS), :]
                s_s[rows, a * t:(a + 1) * t] = part
        shift = 0.0 if near else far_bias
        for r0 in range(0, 2 * t, DIFF_ROWS):
            rows = pl.ds(r0, DIFF_ROWS)
            s = s_s[rows, :width]
            m_cur = jnp.max(s, axis=-1, keepdims=True) + shift
            m_old = m_s[rows, :]
            m_new = jnp.maximum(m_old, m_cur)
            alpha = jnp.exp2(m_old - m_new)
            m_sub = m_new - shift
            p = jnp.exp2(s - jnp.concatenate([m_sub] * (n_tiles * n_rep), axis=1))
            pv = jnp.dot(p.astype(BF16), vx, preferred_element_type=F32)
            acc_s[rows, :] = jnp.concatenate([alpha, alpha], axis=1) * acc_s[rows, :] + pv
            m_s[rows, :] = m_new

    n_far = jnp.maximum(i - (n_near - 1), 0)
    n_quads = n_far // DIFF_MAX_TILES
    j_pairs = DIFF_MAX_TILES * n_quads
    n_far_pairs = (n_far - j_pairs) >> 1
    j_near = j_pairs + 2 * n_far_pairs
    odd = (i + 1 - j_near) & 1

    def far_quad(jj, carry):
        sweep(DIFF_MAX_TILES * jj, DIFF_MAX_TILES, False)
        return carry

    def far_pair(jj, carry):
        sweep(j_pairs + 2 * jj, 2, False)
        return carry

    lax.fori_loop(0, n_quads, far_quad, 0)
    lax.fori_loop(0, n_far_pairs, far_pair, 0)

    @pl.when(odd == 1)
    def _():
        sweep(j_near, 1, True)

    def near_pair(kk, carry):
        sweep(j_near + odd + 2 * kk, 2, True)
        return carry

    lax.fori_loop(0, (i + 1 - j_near) >> 1, near_pair, 0)

    acc = acc_s[...]
    o = acc[:, :LANES] / acc[:, LANES:]
    lam = (jnp.exp(jnp.sum(lam_ref[0:1, :] * lam_ref[1:2, :], axis=-1, keepdims=True))
           - jnp.exp(jnp.sum(lam_ref[2:3, :] * lam_ref[3:4, :], axis=-1, keepdims=True)) + lam_init)
    y = o[:t] - lam * o[t:]
    y = y * lax.rsqrt(jnp.mean(y * y, axis=-1, keepdims=True) + EPS)
    o_ref[...] = (y * g_ref[...] * (1.0 - lam_init)).astype(o_ref.dtype)


def _diff_attention(q, kt, v, bias_c, far_bias, lam_params, g_sub_l, lam_init):
    bn, s, w = q.shape
    t = TILE_M
    n_t = s // t
    n_heads = w // LANES
    n_near = bias_c.shape[1] - 1
    return pl.pallas_call(
        functools.partial(_diff_kernel, lam_init=lam_init, n_near=n_near),
        grid=(bn, n_heads, n_t),
        in_specs=[pl.BlockSpec(memory_space=pltpu.SMEM),
                  pl.BlockSpec((None, t, LANES), lambda b, h, i: (b, i, h)),
                  pl.BlockSpec((None, n_t, LANES, t), lambda b, h, i: (b, 0, h, 0)),
                  pl.BlockSpec((None, s, LANES), lambda b, h, i: (b, 0, h)),
                  pl.BlockSpec((None, n_near + 1, t, t), lambda b, h, i: (h, 0, 0, 0)),
                  pl.BlockSpec((4, HEAD_DIM), lambda b, h, i: (0, 0)),
                  pl.BlockSpec((1, LANES), lambda b, h, i: (0, 0))],
        out_specs=pl.BlockSpec((None, t, LANES), lambda b, h, i: (b, i, h)),
        out_shape=jax.ShapeDtypeStruct((bn, s, w), BF16),
        scratch_shapes=[pltpu.VMEM((2 * t, LANES), BF16),
                        pltpu.VMEM((s, 2 * LANES), BF16),
                        pltpu.VMEM((2 * t, LANES), F32),
                        pltpu.VMEM((2 * t, 2 * LANES), F32),
                        pltpu.VMEM((2 * t, DIFF_MAX_TILES * t), F32)],
        compiler_params=pltpu.CompilerParams(
            dimension_semantics=("parallel", "parallel", "arbitrary"),
            vmem_limit_bytes=VMEM_LIMIT),
        name="diff_attn",
    )(far_bias, q, kt, v, bias_c, lam_params, g_sub_l.reshape(1, LANES))


def _out_proj_kernel(x_ref, mod_ref, g_ref, w_ref, e_ref, ya_ref, o1_ref, o4_ref, o16_ref,
                     l1_ref, l4_ref, l16_ref, yc_ref, z_ref, out_ref,
                     o4_scr, o16_scr, l4_scr, l16_scr):
    d_model = x_ref.shape[-1]
    wa = ya_ref.shape[-1]
    wb = o1_ref.shape[-1]
    z = z_ref[...].astype(F32)
    sz = z / (1.0 + jnp.exp(-z))

    def to_natural(cm_ref, scr):
        n_cls, per_cls = cm_ref.shape[0], cm_ref.shape[1]
        for k in range(scr.shape[0]):
            for c in range(n_cls):
                scr[k, pl.ds(_class_residue(c), per_cls, stride=n_cls), :] = cm_ref[
                    c, :, k * LANES:(k + 1) * LANES].astype(F32)
        return jnp.concatenate([scr[k] for k in range(scr.shape[0])], axis=1)

    o_parts = [o1_ref[...].astype(F32), to_natural(o4_ref, o4_scr), to_natural(o16_ref, o16_scr)]
    stats = [l1_ref[...], to_natural(l4_ref, l4_scr), to_natural(l16_ref, l16_scr)]
    head_lane = lax.broadcasted_iota(jnp.int32, stats[0].shape, 1) < wb // HEAD_DIM
    maxes = [jnp.where(head_lane, st, 0.0) for st in stats]
    dens = [jnp.where(head_lane, pltpu.roll(st, HEAD_DIM, 1), 1.0) for st in stats]
    lses = [mxp + jnp.log2(dn) for mxp, dn in zip(maxes, dens)]
    mx = jnp.maximum(jnp.maximum(lses[0], lses[1]), lses[2])
    ws = [jnp.exp2(l - mx) for l in lses]
    tot = ws[0] + ws[1] + ws[2]
    yb = jnp.zeros((x_ref.shape[0], wb), F32)
    for wgt, dn, o_part in zip(ws, dens, o_parts):
        a_hi, a_lo = _split_bf16(wgt / (tot * dn))
        spread = (jnp.dot(a_hi, e_ref[...], preferred_element_type=F32)
                  + jnp.dot(a_lo, e_ref[...], preferred_element_type=F32))
        yb = yb + spread * o_part

    ga = (ya_ref[...].astype(F32) * sz[:, :wa]).astype(BF16)
    gb = (yb * sz[:, wa:wa + wb]).astype(BF16)
    gc = (yc_ref[...].astype(F32) * sz[:, wa + wb:]).astype(BF16)
    y = jnp.dot(ga, w_ref[:wa], preferred_element_type=F32)
    y += jnp.dot(gb, w_ref[wa:wa + wb], preferred_element_type=F32)
    y += jnp.dot(gc, w_ref[wa + wb:], preferred_element_type=F32)
    r = (y * lax.rsqrt(jnp.mean(y * y, axis=-1, keepdims=True) + EPS)) * g_ref[...]
    gate = mod_ref[:, 2 * d_model:]
    out_ref[...] = x_ref[...] + gate * r


def _out_proj(x, mod_l, g_post_l, w_out_l, spread, ya, ob, lb, yc, z):
    bn, s, d_model = x.shape
    tm = TILE_M
    per_cls = tm // N_CLASSES
    wb = ob[0].shape[-1]
    row = lambda w: pl.BlockSpec((None, tm, w), lambda b, i: (b, i, 0))
    cm = lambda w: pl.BlockSpec((None, N_CLASSES, per_cls, w), lambda b, i: (b, 0, i, 0))
    whole = lambda a: pl.BlockSpec(a.shape, lambda b, i: (0,) * a.ndim,
                                   pipeline_mode=pl.Buffered(1))
    g2 = g_post_l.reshape(1, d_model)
    return pl.pallas_call(
        _out_proj_kernel,
        grid=(bn, s // tm),
        in_specs=[row(d_model),
                  pl.BlockSpec((None, 1, 3 * d_model), lambda b, i: (b, 0, 0)),
                  whole(g2), whole(w_out_l), whole(spread),
                  row(ya.shape[-1]),
                  row(wb), cm(wb), cm(wb),
                  row(LANES), cm(LANES), cm(LANES),
                  row(yc.shape[-1]), row(z.shape[-1])],
        out_specs=row(d_model),
        out_shape=jax.ShapeDtypeStruct(x.shape, x.dtype),
        scratch_shapes=[pltpu.VMEM((wb // LANES, tm, LANES), F32),
                        pltpu.VMEM((wb // LANES, tm, LANES), F32),
                        pltpu.VMEM((1, tm, LANES), F32), pltpu.VMEM((1, tm, LANES), F32)],
        compiler_params=pltpu.CompilerParams(
            dimension_semantics=("parallel", "parallel"), vmem_limit_bytes=VMEM_LIMIT),
        name="out_proj",
    )(x, mod_l.reshape(bn, 1, 3 * d_model), g2, w_out_l, spread, ya, *ob, *lb, yc, z)


def _prepare_w_in(w, sizes, n_kv_a):
    offs = np.concatenate([[0], np.cumsum(sizes)])
    grp = lambda i: w[:, int(offs[i]):int(offs[i + 1])]
    aq, ak, av, bq, bk, bv, cq, ck, cv, z = [grp(i) for i in range(10)]

    def dup(m):
        parts = []
        for g in range(n_kv_a):
            head = m[:, g * HEAD_DIM:(g + 1) * HEAD_DIM]
            parts += [head, head]
        return jnp.concatenate(parts, axis=1)

    groups = [aq * Q_SCALE, dup(ak), dup(av), bq * Q_SCALE, bk, bv, cq * Q_SCALE, cv, z]
    widths = tuple(int(g.shape[1]) for g in groups)
    w_cols = jnp.concatenate(groups, axis=1).astype(BF16)
    return w_cols, ck.T.astype(BF16), widths


def kernel(x, c, rel_table, w_in, w_out, w_ada, b_ada, g_pre, g_post, a_sinks,
           lam_q1, lam_k1, lam_q2, lam_k2, g_sub):
    bn, s, d_model = x.shape
    depth = w_in.shape[0]
    d_mix = w_out.shape[1]
    a_width = b_width = 3 * d_mix // 8
    c_width = d_mix // 4
    n_a_heads = a_width // HEAD_DIM
    n_kv_a = n_a_heads // 4
    n_b_heads = b_width // HEAD_DIM
    n_c_heads = c_width // (2 * HEAD_DIM)
    sizes = (a_width, n_kv_a * HEAD_DIM, n_kv_a * HEAD_DIM, b_width, b_width, b_width,
             c_width, c_width, c_width, d_mix)
    assert w_in.shape[2] == sum(sizes) and rel_table.shape == (REL_BUCKETS,
                                                                n_a_heads + n_b_heads + n_c_heads)
    assert n_a_heads == n_b_heads and s % (N_CLASSES * BLK) == 0
    assert [d for _, d in B_PATTERNS] == [1, BAND_PIECES, N_CLASSES]
    assert s % TILE_M == 0 and TILE_M % BLK == 0

    n_near = -(-(FAR_DIST - 1) // TILE_M) + 1
    bias_a, bias_b, bias_c = _bias_tables(rel_table, n_a_heads, n_b_heads, n_c_heads,
                                          TILE_M, n_near + 1)
    far_bias = rel_table[REL_BUCKETS - 1, n_a_heads + n_b_heads:]
    mod = _modulation(c, w_ada, b_ada)

    head_of_lane = np.arange(b_width) // HEAD_DIM
    spread = jnp.asarray(np.arange(LANES)[:, None] == head_of_lane[None, :], BF16)

    for l in range(depth):
        w_cols, w_kt, widths = _prepare_w_in(w_in[l], sizes, n_kv_a)
        (aq, ak, av, bq, bk, bv, cq, cv, z, bq_cm, bk_cm, bv_cm, ckt) = _in_proj(
            x, mod[l], g_pre[l], w_cols, w_kt, widths, cm_groups=(3, 4, 5))

        ya = _banded_attention(aq, ak, av, bias_a, 2, "natural", sinks=a_sinks[l])
        o1, l1 = _banded_attention(bq, bk, bv, bias_b[0], 1, "natural")
        o4, l4 = _banded_attention(bq_cm, bk_cm, bv_cm, bias_b[1], 1, "pieces")
        o16, l16 = _banded_attention(bq_cm, bk_cm, bv_cm, bias_b[2], 1, "classes")
        ob, lb = (o1, o4, o16), (l1, l4, l16)

        lam_init = 0.8 - 0.6 * math.exp(-0.3 * l)
        lam_params = jnp.stack([lam_q1[l], lam_k1[l], lam_q2[l], lam_k2[l]]).astype(F32)
        yc = _diff_attention(cq, ckt, cv, bias_c, far_bias, lam_params, g_sub[l], lam_init)

        x = _out_proj(x, mod[l], g_post[l], w_out[l].astype(BF16), spread, ya, ob, lb, yc, z)
    return x
```

```python
import functools
import math

import numpy as np
import jax
import jax.numpy as jnp
from jax import lax
from jax.experimental import pallas as pl
from jax.experimental.pallas import tpu as pltpu

F32 = jnp.float32
BF16 = jnp.bfloat16

HEAD_DIM = 64
LANES = 128
BLK = 128
A_WINDOW = 128
B_PATTERNS = ((128, 1), (512, 4), (2048, 16))
REL_BUCKETS = 32
REL_MAX_DIST = 2048
EPS = 1e-6
NEG = -1e30
LOG2E = 1.4426950408889634
Q_SCALE = LOG2E / math.sqrt(HEAD_DIM)

TILE_M = 512
BAND_BLOCKS = 8
N_CLASSES = 16
BAND_PIECES = 4
PIECED_PATTERN = 1
DIFF_ROWS = 128
DIFF_MAX_TILES = 4
VMEM_LIMIT = 56 * 1024 * 1024


def _class_residue(c):
    return c // BAND_PIECES + (N_CLASSES // BAND_PIECES) * (c % BAND_PIECES)


def _rel_bucket_np(n):
    n = np.maximum(np.asarray(n, np.int64), 0)
    max_exact = REL_BUCKETS // 2
    nf = np.maximum(n, 1).astype(np.float32)
    large = max_exact + (np.log(nf / np.float32(max_exact))
                         / np.float32(math.log(REL_MAX_DIST / max_exact))
                         * np.float32(REL_BUCKETS - max_exact)).astype(np.int32)
    large = np.minimum(large, REL_BUCKETS - 1)
    return np.where(n < max_exact, n, large)


_BUCKET_OF = _rel_bucket_np(np.arange(0, 1 << 18))
_THR = [0] + [int(np.argmax(_BUCKET_OF >= b)) for b in range(1, REL_BUCKETS)]
FAR_DIST = _THR[REL_BUCKETS - 1]


def _bucket_values(dist, dist_scale, lo, hi, table_ref, col):
    b_lo = int(_BUCKET_OF[max(lo, 0) * dist_scale])
    b_hi = int(_BUCKET_OF[max(hi, 0) * dist_scale])
    val = jnp.full(dist.shape, table_ref[b_lo, col] * LOG2E, F32)
    for b in range(b_lo + 1, b_hi + 1):
        thr = -(-_THR[b] // dist_scale)
        val = jnp.where(dist >= thr, table_ref[b, col] * LOG2E, val)
    return val


def _band_bias_kernel(table_ref, a_ref, b_ref):
    h = pl.program_id(0)
    n_a_heads = pl.num_programs(0)
    row = lax.broadcasted_iota(jnp.int32, (BLK, 2 * BLK), 0)
    col = lax.broadcasted_iota(jnp.int32, (BLK, 2 * BLK), 1)
    first_ok = col >= BLK

    def pieced(idx):
        per = BLK // BAND_PIECES
        return (idx & (per - 1)) * BAND_PIECES + (idx >> (per.bit_length() - 1))

    def tile(dist, max_dist, dist_scale, table_col):
        val = _bucket_values(dist, dist_scale, 0, max_dist, table_ref, table_col)
        val = jnp.where((dist >= 0) & (dist <= max_dist), val, NEG)
        return jnp.where(first_ok, val, NEG), val

    dist = row + BLK - col
    a_ref[0], a_ref[1] = tile(dist, A_WINDOW - 1, 1, h)
    for p, (w, d) in enumerate(B_PATTERNS):
        if p == PIECED_PATTERN:
            within = col & (BLK - 1)
            dist_p = pieced(row) + BLK - (pieced(within) + (col - within))
        else:
            dist_p = dist
        b_ref[p, 0], b_ref[p, 1] = tile(dist_p, w // d, d, n_a_heads + h)


def _diff_bias_kernel(table_ref, out_ref, *, tile, n_tiles, head_offset):
    h = pl.program_id(0)
    row = lax.broadcasted_iota(jnp.int32, (BLK, BLK), 0)
    col = lax.broadcasted_iota(jnp.int32, (BLK, BLK), 1)
    sub = tile // BLK
    pieces = {}
    for delta in range(-(sub - 1), n_tiles * sub):
        if delta < 0:
            pieces[delta] = jnp.full((BLK, BLK), NEG, F32)
            continue
        dist = row - col + delta * BLK
        val = _bucket_values(dist, 1, delta * BLK - (BLK - 1), delta * BLK + BLK - 1,
                             table_ref, head_offset + h)
        pieces[delta] = jnp.where(dist >= 0, val, NEG) if delta == 0 else val
    for dt in range(n_tiles):
        for a in range(sub):
            for b in range(sub):
                out_ref[dt, a * BLK:(a + 1) * BLK, b * BLK:(b + 1) * BLK] = pieces[dt * sub + a - b]


def _bias_tables(rel_table, n_a_heads, n_b_heads, n_c_heads, tile, n_tiles):
    smem = pl.BlockSpec(memory_space=pltpu.SMEM)
    bias_a, bias_b = pl.pallas_call(
        _band_bias_kernel,
        grid=(n_a_heads,),
        in_specs=[smem],
        out_specs=[pl.BlockSpec((2, BLK, 2 * BLK), lambda h: (0, h, 0)),
                   pl.BlockSpec((3, 2, BLK, 2 * BLK), lambda h: (0, 0, h, 0))],
        out_shape=[jax.ShapeDtypeStruct((2, n_a_heads * BLK, 2 * BLK), F32),
                   jax.ShapeDtypeStruct((3, 2, n_b_heads * BLK, 2 * BLK), F32)],
        name="band_bias",
    )(rel_table)
    bias_c = pl.pallas_call(
        functools.partial(_diff_bias_kernel, tile=tile, n_tiles=n_tiles,
                          head_offset=n_a_heads + n_b_heads),
        grid=(n_c_heads,),
        in_specs=[smem],
        out_specs=pl.BlockSpec((None, n_tiles, tile, tile), lambda h: (h, 0, 0, 0)),
        out_shape=jax.ShapeDtypeStruct((n_c_heads, n_tiles, tile, tile), F32),
        compiler_params=pltpu.CompilerParams(vmem_limit_bytes=VMEM_LIMIT),
        name="diff_bias",
    )(rel_table)
    return bias_a, bias_b, bias_c


def _split_bf16(v):
    hi = v.astype(BF16)
    lo = (v - hi.astype(F32)).astype(BF16)
    return hi, lo


def _mod_kernel(c_ref, w_ref, b_ref, out_ref):
    c = c_ref[...]
    s = c / (1.0 + jnp.exp(-c))
    s_hi, s_lo = _split_bf16(s)
    w_hi, w_lo = _split_bf16(w_ref[...])
    acc = jnp.dot(s_hi, w_hi, preferred_element_type=F32)
    acc += jnp.dot(s_lo, w_hi, preferred_element_type=F32)
    acc += jnp.dot(s_hi, w_lo, preferred_element_type=F32)
    out_ref[...] = acc + b_ref[...]


def _modulation(c, w_ada, b_ada):
    depth, d_model, n_out = w_ada.shape
    rows = 8
    c_pad = jnp.zeros((rows, d_model), F32).at[:c.shape[0]].set(c)
    tn = 768
    out = pl.pallas_call(
        _mod_kernel,
        grid=(depth, n_out // tn),
        in_specs=[pl.BlockSpec((rows, d_model), lambda l, j: (0, 0)),
                  pl.BlockSpec((None, d_model, tn), lambda l, j: (l, 0, j)),
                  pl.BlockSpec((None, 1, tn), lambda l, j: (l, 0, j))],
        out_specs=pl.BlockSpec((None, rows, tn), lambda l, j: (l, 0, j)),
        out_shape=jax.ShapeDtypeStruct((depth, rows, n_out), F32),
        name="adaln_mod",
    )(c_pad, w_ada, b_ada.reshape(depth, 1, n_out))
    return out[:, :c.shape[0]]


def _in_proj_kernel(x_ref, mod_ref, g_ref, w_ref, wkt_ref, *refs, widths, cm_groups):
    n_out = len(widths)
    col_refs = refs[:n_out]
    cm_refs = refs[n_out:n_out + len(cm_groups)]
    kt_ref = refs[n_out + len(cm_groups)]
    scr_refs = refs[n_out + len(cm_groups) + 1:]
    d_model = x_ref.shape[-1]
    x = x_ref[...]
    ms = jnp.mean(x * x, axis=-1, keepdims=True)
    shift = mod_ref[:, :d_model]
    scale = mod_ref[:, d_model:2 * d_model]
    h = (x * lax.rsqrt(ms + EPS)) * g_ref[...]
    h = (h * (1.0 + scale) + shift).astype(BF16)
    off = 0
    for gi, (ref, width) in enumerate(zip(col_refs, widths)):
        res = jnp.dot(h, w_ref[:, off:off + width], preferred_element_type=F32)
        ref[...] = res.astype(ref.dtype)
        if gi in cm_groups:
            cm_ref, scr = cm_refs[cm_groups.index(gi)], scr_refs[cm_groups.index(gi)]
            n_cls, per_cls = cm_ref.shape[0], cm_ref.shape[1]
            for k in range(width // LANES):
                lanes = slice(k * LANES, (k + 1) * LANES)
                scr[k] = res[:, lanes]
                for c in range(n_cls):
                    cm_ref[c, :, lanes] = scr[
                        k, pl.ds(_class_residue(c), per_cls, stride=n_cls), :].astype(cm_ref.dtype)
        off += width
    kt_ref[...] = lax.dot_general(wkt_ref[...], h, (((1,), (1,)), ((), ())),
                                  preferred_element_type=F32).astype(kt_ref.dtype)


def _in_proj(x, mod_l, g_pre_l, w_cols, w_kt, widths, cm_groups):
    bn, s, d_model = x.shape
    tm = TILE_M
    n_t = s // tm
    n_cols = w_cols.shape[1]
    kt_rows = w_kt.shape[0]
    per_cls = tm // N_CLASSES
    out_shape = [jax.ShapeDtypeStruct((bn, s, w), BF16) for w in widths]
    out_specs = [pl.BlockSpec((None, tm, w), lambda b, i: (b, i, 0)) for w in widths]
    for gi in cm_groups:
        out_shape.append(jax.ShapeDtypeStruct((bn, N_CLASSES, s // N_CLASSES, widths[gi]), BF16))
        out_specs.append(pl.BlockSpec((None, N_CLASSES, per_cls, widths[gi]),
                                      lambda b, i: (b, 0, i, 0)))
    out_shape.append(jax.ShapeDtypeStruct((bn, n_t, kt_rows, tm), BF16))
    out_specs.append(pl.BlockSpec((None, None, kt_rows, tm), lambda b, i: (b, i, 0, 0)))
    return pl.pallas_call(
        functools.partial(_in_proj_kernel, widths=widths, cm_groups=cm_groups),
        grid=(bn, n_t),
        in_specs=[pl.BlockSpec((None, tm, d_model), lambda b, i: (b, i, 0)),
                  pl.BlockSpec((None, 1, 3 * d_model), lambda b, i: (b, 0, 0)),
                  pl.BlockSpec((1, d_model), lambda b, i: (0, 0)),
                  pl.BlockSpec((d_model, n_cols), lambda b, i: (0, 0),
                               pipeline_mode=pl.Buffered(1)),
                  pl.BlockSpec((kt_rows, d_model), lambda b, i: (0, 0),
                               pipeline_mode=pl.Buffered(1))],
        out_specs=out_specs,
        out_shape=out_shape,
        scratch_shapes=[pltpu.VMEM((widths[gi] // LANES, tm, LANES), F32) for gi in cm_groups],
        compiler_params=pltpu.CompilerParams(
            dimension_semantics=("parallel", "parallel"), vmem_limit_bytes=VMEM_LIMIT),
        name="in_proj",
    )(x, mod_l.reshape(bn, 1, 3 * d_model), g_pre_l.reshape(1, d_model), w_cols, w_kt)


def _lane_masks():
    lane = lax.broadcasted_iota(jnp.int32, (1, LANES), 1)
    even = lane < HEAD_DIM
    m_even = jnp.where(even, 1.0, 0.0).astype(BF16)
    m_odd = jnp.where(even, 0.0, 1.0).astype(BF16)
    return even, m_even, m_odd


def _banded_kernel(*refs, n_pairs, pairs_per_kv, has_sink, pieces):
    it = iter(refs)
    sink_ref = next(it) if has_sink else None
    q_ref, kp_ref, kc_ref, vp_ref, vc_ref, bias_ref, o_ref = (next(it) for _ in range(7))
    lse_ref = None if has_sink else next(it)
    k_buf, v_buf, s_buf = next(it), next(it), next(it)
    step = pl.program_id(2)
    if pieces == 1:
        nblk = q_ref.shape[0] // BLK
        k_buf[:BLK] = kp_ref[...]
        k_buf[BLK:] = kc_ref[...]
        v_buf[:BLK] = vp_ref[...]
        v_buf[BLK:] = vc_ref[...]
        q_src, o_dst, lse_dst = q_ref, o_ref, lse_ref
    else:
        q_src, o_dst, lse_dst = next(it), next(it), next(it)
        per = BLK // pieces
        nblk = q_ref.shape[1] // per
        for pc in range(pieces):
            k_buf[pc * per:(pc + 1) * per] = kp_ref[pc]
            v_buf[pc * per:(pc + 1) * per] = vp_ref[pc]
            for ib in range(nblk):
                dst = (ib * pieces + pc) * per
                src = slice(ib * per, (ib + 1) * per)
                q_src[dst:dst + per] = q_ref[pc, src]
                k_buf[BLK + dst:BLK + dst + per] = kc_ref[pc, src]
                v_buf[BLK + dst:BLK + dst + per] = vc_ref[pc, src]
    even, m_even, m_odd = _lane_masks()
    lane = lax.broadcasted_iota(jnp.int32, (BLK, LANES), 1)
    top_rows = lax.broadcasted_iota(jnp.int32, (2 * BLK, 1), 0) < BLK
    ones = jnp.ones((2 * BLK, LANES), BF16)

    for ib in range(nblk):
        r0 = ib * BLK
        variant = jnp.where(step == 0, 0, 1) if ib == 0 else 1
        lse_tile = jnp.ones((BLK, LANES), F32)
        for j in range(n_pairs):
            g = j // pairs_per_kv
            q2 = q_src[pl.ds(r0, BLK), j * LANES:(j + 1) * LANES]
            qs = jnp.concatenate([q2 * m_even, q2 * m_odd], axis=0)
            k2 = k_buf[pl.ds(r0, 2 * BLK), g * LANES:(g + 1) * LANES]
            s = lax.dot_general(qs, k2, (((1,), (1,)), ((), ())), preferred_element_type=F32)
            s_buf[ib * n_pairs + j] = s + bias_ref[variant, j * 2 * BLK:(j + 1) * 2 * BLK, :]
        for j in range(n_pairs):
            g = j // pairs_per_kv
            v2 = v_buf[pl.ds(r0, 2 * BLK), g * LANES:(g + 1) * LANES]
            s = s_buf[ib * n_pairs + j]
            m = jnp.max(s, axis=-1, keepdims=True)
            if has_sink:
                sk = jnp.where(top_rows, sink_ref[2 * j], sink_ref[2 * j + 1]) * LOG2E
                m = jnp.maximum(m, sk)
            e = jnp.exp2(s - m)
            ox = jnp.dot(e.astype(BF16), jnp.concatenate([v2, ones], axis=1),
                         preferred_element_type=F32)
            ov, den = ox[:, :LANES], ox[:, LANES:]
            if has_sink:
                ov = ov / (den + jnp.exp2(sk - m))
            o_dst[pl.ds(r0, BLK), j * LANES:(j + 1) * LANES] = (
                jnp.where(even, ov[:BLK], ov[BLK:]).astype(o_dst.dtype))
            if lse_dst is not None:
                for hh, half in ((2 * j, slice(0, BLK)), (2 * j + 1, slice(BLK, 2 * BLK))):
                    lse_tile = jnp.where(lane == hh, m[half], lse_tile)
                    lse_tile = jnp.where(lane == HEAD_DIM + hh, den[half], lse_tile)
        if lse_dst is not None:
            lse_dst[pl.ds(r0, BLK), :] = lse_tile

    if pieces > 1:
        for pc in range(pieces):
            for ib in range(nblk):
                dst = (ib * pieces + pc) * per
                o_ref[pc, ib * per:(ib + 1) * per] = o_dst[dst:dst + per]
                lse_ref[pc, ib * per:(ib + 1) * per] = lse_dst[dst:dst + per]


def _banded_attention(q, k, v, bias, pairs_per_kv, layout, sinks=None):
    wq, wkv = q.shape[-1], k.shape[-1]
    bn = q.shape[0]
    n_pairs = wq // LANES
    has_sink = sinks is not None
    if layout == "natural":
        seq = q.shape[1]
        nblk = min(BAND_BLOCKS, seq // BLK)
        rows = nblk * BLK
        grid = (bn, 1, seq // rows)
        cur = lambda b, r, n: (b, n, 0)
        prev = lambda b, r, n: (b, jnp.maximum(n * nblk - 1, 0), 0)
        cur_blk = lambda w: (None, rows, w)
        prev_blk = lambda w: (None, BLK, w)
        pieces = 1
    elif layout == "classes":
        n_cls, seq = q.shape[1], q.shape[2]
        nblk = min(BAND_BLOCKS, seq // BLK)
        rows = nblk * BLK
        grid = (bn, n_cls, seq // rows)
        cur = lambda b, r, n: (b, r, n, 0)
        prev = lambda b, r, n: (b, r, jnp.maximum(n * nblk - 1, 0), 0)
        cur_blk = lambda w: (None, None, rows, w)
        prev_blk = lambda w: (None, None, BLK, w)
        pieces = 1
    else:
        n_cls, seq = q.shape[1], q.shape[2]
        pieces = BAND_PIECES
        per = BLK // pieces
        nblk = pieces
        rows = BLK
        grid = (bn, n_cls // pieces, seq // rows)
        cur = lambda b, r, n: (b, r, n, 0)
        prev = lambda b, r, n: (b, r, jnp.maximum(n * (rows // per) - 1, 0), 0)
        cur_blk = lambda w: (None, pieces, rows, w)
        prev_blk = lambda w: (None, pieces, per, w)
    tot = nblk * BLK
    in_specs = [pl.BlockSpec(cur_blk(wq), cur),
                pl.BlockSpec(prev_blk(wkv), prev),
                pl.BlockSpec(cur_blk(wkv), cur),
                pl.BlockSpec(prev_blk(wkv), prev),
                pl.BlockSpec(cur_blk(wkv), cur),
                pl.BlockSpec(bias.shape, lambda b, r, n: (0, 0, 0))]
    args = [q, k, k, v, v, bias]
    out_specs = [pl.BlockSpec(cur_blk(wq), cur)]
    out_shape = [jax.ShapeDtypeStruct(q.shape, BF16)]
    scratch = [pltpu.VMEM((tot + BLK, wkv), BF16), pltpu.VMEM((tot + BLK, wkv), BF16),
               pltpu.VMEM((nblk * n_pairs, 2 * BLK, 2 * BLK), F32)]
    if has_sink:
        in_specs.insert(0, pl.BlockSpec(memory_space=pltpu.SMEM))
        args.insert(0, sinks)
    else:
        out_specs.append(pl.BlockSpec(cur_blk(LANES), cur))
        out_shape.append(jax.ShapeDtypeStruct(q.shape[:-1] + (LANES,), F32))
    if pieces > 1:
        scratch += [pltpu.VMEM((tot, wq), BF16), pltpu.VMEM((tot, wq), BF16),
                    pltpu.VMEM((tot, LANES), F32)]
    outs = pl.pallas_call(
        functools.partial(_banded_kernel, n_pairs=n_pairs, pairs_per_kv=pairs_per_kv,
                          has_sink=has_sink, pieces=pieces),
        grid=grid,
        in_specs=in_specs,
        out_specs=out_specs,
        out_shape=out_shape,
        scratch_shapes=scratch,
        compiler_params=pltpu.CompilerParams(
            dimension_semantics=("parallel", "parallel", "arbitrary"),
            vmem_limit_bytes=VMEM_LIMIT),
        name="banded_sink" if has_sink else f"banded_{layout}",
    )(*args)
    return outs[0] if has_sink else tuple(outs)


def _diff_kernel(far_ref, q_ref, kt_ref, v_ref, bias_ref, lam_ref, g_ref, o_ref,
                 qs_s, vx_s, m_s, acc_s, s_s, *, lam_init, n_near):
    h = pl.program_id(1)
    i = pl.program_id(2)
    t = q_ref.shape[0]
    n_rep = t // LANES

    @pl.when(i == 0)
    def _():
        vx_s[:, :LANES] = v_ref[...]
        vx_s[:, LANES:] = jnp.ones((vx_s.shape[0], LANES), BF16)

    _, m_even, m_odd = _lane_masks()
    q2 = q_ref[...]
    qs_s[:t] = q2 * m_even
    qs_s[t:] = q2 * m_odd
    m_s[...] = jnp.full(m_s.shape, NEG, F32)
    acc_s[...] = jnp.zeros(acc_s.shape, F32)
    far_bias = far_ref[h] * LOG2E

    def sweep(j0, n_tiles, near):
        k0 = pl.multiple_of(j0 * t, t)
        kts = [kt_ref[j0 + a] for a in range(n_tiles)]
        vx = vx_s[pl.ds(k0, n_tiles * t), :]
        width = n_tiles * t
        for r0 in range(0, 2 * t, DIFF_ROWS):
            rows = pl.ds(r0, DIFF_ROWS)
            qc = qs_s[rows, :]
            for a, kt in enumerate(kts):
                part = jnp.dot(qc, kt, preferred_element_type=F32)
                if near:
                    part = part + bias_ref[jnp.minimum(i - j0 - a, n_near),
                                           pl.ds(r0 % t, DIFF_ROWS), :]
                s_s[rows, a * t:(a + 1) * t] = part
        shift = 0.0 if near else far_bias
        for r0 in range(0, 2 * t, DIFF_ROWS):
            rows = pl.ds(r0, DIFF_ROWS)
            s = s_s[rows, :width]
            m_cur = jnp.max(s, axis=-1, keepdims=True) + shift
            m_old = m_s[rows, :]
            m_new = jnp.maximum(m_old, m_cur)
            alpha = jnp.exp2(m_old - m_new)
            m_sub = m_new - shift
            p = jnp.exp2(s - jnp.concatenate([m_sub] * (n_tiles * n_rep), axis=1))
            pv = jnp.dot(p.astype(BF16), vx, preferred_element_type=F32)
            acc_s[rows, :] = jnp.concatenate([alpha, alpha], axis=1) * acc_s[rows, :] + pv
            m_s[rows, :] = m_new

    n_far = jnp.maximum(i - (n_near - 1), 0)
    n_quads = n_far // DIFF_MAX_TILES
    j_pairs = DIFF_MAX_TILES * n_quads
    n_far_pairs = (n_far - j_pairs) >> 1
    j_near0 = j_pairs + 2 * n_far_pairs
    n_near_quads = jnp.where(i + 1 - j_near0 >= DIFF_MAX_TILES, 1, 0)
    j_near = j_near0 + DIFF_MAX_TILES * n_near_quads
    odd = (i + 1 - j_near) & 1

    def far_quad(jj, carry):
        sweep(DIFF_MAX_TILES * jj, DIFF_MAX_TILES, False)
        return carry

    def far_pair(jj, carry):
        sweep(j_pairs + 2 * jj, 2, False)
        return carry

    def near_quad(jj, carry):
        sweep(j_near0, DIFF_MAX_TILES, True)
        return carry

    lax.fori_loop(0, n_quads, far_quad, 0)
    lax.fori_loop(0, n_far_pairs, far_pair, 0)
    lax.fori_loop(0, n_near_quads, near_quad, 0)

    @pl.when(odd == 1)
    def _():
        sweep(j_near, 1, True)

    def near_pair(kk, carry):
        sweep(j_near + odd + 2 * kk, 2, True)
        return carry

    lax.fori_loop(0, (i + 1 - j_near) >> 1, near_pair, 0)

    acc = acc_s[...]
    o = acc[:, :LANES] / acc[:, LANES:]
    lam = (jnp.exp(jnp.sum(lam_ref[0:1, :] * lam_ref[1:2, :], axis=-1, keepdims=True))
           - jnp.exp(jnp.sum(lam_ref[2:3, :] * lam_ref[3:4, :], axis=-1, keepdims=True)) + lam_init)
    y = o[:t] - lam * o[t:]
    y = y * lax.rsqrt(jnp.mean(y * y, axis=-1, keepdims=True) + EPS)
    o_ref[...] = (y * g_ref[...] * (1.0 - lam_init)).astype(o_ref.dtype)


def _diff_attention(q, kt, v, bias_c, far_bias, lam_params, g_sub_l, lam_init):
    bn, s, w = q.shape
    t = TILE_M
    n_t = s // t
    n_heads = w // LANES
    n_near = bias_c.shape[1] - 1
    return pl.pallas_call(
        functools.partial(_diff_kernel, lam_init=lam_init, n_near=n_near),
        grid=(bn, n_heads, n_t),
        in_specs=[pl.BlockSpec(memory_space=pltpu.SMEM),
                  pl.BlockSpec((None, t, LANES), lambda b, h, i: (b, i, h)),
                  pl.BlockSpec((None, n_t, LANES, t), lambda b, h, i: (b, 0, h, 0)),
                  pl.BlockSpec((None, s, LANES), lambda b, h, i: (b, 0, h)),
                  pl.BlockSpec((None, n_near + 1, t, t), lambda b, h, i: (h, 0, 0, 0)),
                  pl.BlockSpec((4, HEAD_DIM), lambda b, h, i: (0, 0)),
                  pl.BlockSpec((1, LANES), lambda b, h, i: (0, 0))],
        out_specs=pl.BlockSpec((None, t, LANES), lambda b, h, i: (b, i, h)),
        out_shape=jax.ShapeDtypeStruct((bn, s, w), BF16),
        scratch_shapes=[pltpu.VMEM((2 * t, LANES), BF16),
                        pltpu.VMEM((s, 2 * LANES), BF16),
                        pltpu.VMEM((2 * t, LANES), F32),
                        pltpu.VMEM((2 * t, 2 * LANES), F32),
                        pltpu.VMEM((2 * t, DIFF_MAX_TILES * t), F32)],
        compiler_params=pltpu.CompilerParams(
            dimension_semantics=("parallel", "parallel", "arbitrary"),
            vmem_limit_bytes=VMEM_LIMIT),
        name="diff_attn",
    )(far_bias, q, kt, v, bias_c, lam_params, g_sub_l.reshape(1, LANES))


def _out_proj_kernel(x_ref, mod_ref, g_ref, w_ref, e_ref, ya_ref, o1_ref, o4_ref, o16_ref,
                     l1_ref, l4_ref, l16_ref, yc_ref, z_ref, out_ref,
                     o4_scr, o16_scr, l4_scr, l16_scr):
    d_model = x_ref.shape[-1]
    wa = ya_ref.shape[-1]
    wb = o1_ref.shape[-1]
    z = z_ref[...].astype(F32)
    sz = z / (1.0 + jnp.exp(-z))

    def to_natural(cm_ref, scr):
        n_cls, per_cls = cm_ref.shape[0], cm_ref.shape[1]
        for k in range(scr.shape[0]):
            for c in range(n_cls):
                scr[k, pl.ds(_class_residue(c), per_cls, stride=n_cls), :] = cm_ref[
                    c, :, k * LANES:(k + 1) * LANES].astype(F32)
        return jnp.concatenate([scr[k] for k in range(scr.shape[0])], axis=1)

    o_parts = [o1_ref[...].astype(F32), to_natural(o4_ref, o4_scr), to_natural(o16_ref, o16_scr)]
    stats = [l1_ref[...], to_natural(l4_ref, l4_scr), to_natural(l16_ref, l16_scr)]
    head_lane = lax.broadcasted_iota(jnp.int32, stats[0].shape, 1) < wb // HEAD_DIM
    maxes = [jnp.where(head_lane, st, 0.0) for st in stats]
    dens = [jnp.where(head_lane, pltpu.roll(st, HEAD_DIM, 1), 1.0) for st in stats]
    lses = [mxp + jnp.log2(dn) for mxp, dn in zip(maxes, dens)]
    mx = jnp.maximum(jnp.maximum(lses[0], lses[1]), lses[2])
    ws = [jnp.exp2(l - mx) for l in lses]
    tot = ws[0] + ws[1] + ws[2]
    yb = jnp.zeros((x_ref.shape[0], wb), F32)
    for wgt, dn, o_part in zip(ws, dens, o_parts):
        a_hi, a_lo = _split_bf16(wgt / (tot * dn))
        spread = (jnp.dot(a_hi, e_ref[...], preferred_element_type=F32)
                  + jnp.dot(a_lo, e_ref[...], preferred_element_type=F32))
        yb = yb + spread * o_part

    ga = (ya_ref[...].astype(F32) * sz[:, :wa]).astype(BF16)
    gb = (yb * sz[:, wa:wa + wb]).astype(BF16)
    gc = (yc_ref[...].astype(F32) * sz[:, wa + wb:]).astype(BF16)
    y = jnp.dot(ga, w_ref[:wa], preferred_element_type=F32)
    y += jnp.dot(gb, w_ref[wa:wa + wb], preferred_element_type=F32)
    y += jnp.dot(gc, w_ref[wa + wb:], preferred_element_type=F32)
    r = (y * lax.rsqrt(jnp.mean(y * y, axis=-1, keepdims=True) + EPS)) * g_ref[...]
    gate = mod_ref[:, 2 * d_model:]
    out_ref[...] = x_ref[...] + gate * r


def _out_proj(x, mod_l, g_post_l, w_out_l, spread, ya, ob, lb, yc, z):
    bn, s, d_model = x.shape
    tm = TILE_M
    per_cls = tm // N_CLASSES
    wb = ob[0].shape[-1]
    row = lambda w: pl.BlockSpec((None, tm, w), lambda b, i: (b, i, 0))
    cm = lambda w: pl.BlockSpec((None, N_CLASSES, per_cls, w), lambda b, i: (b, 0, i, 0))
    whole = lambda a: pl.BlockSpec(a.shape, lambda b, i: (0,) * a.ndim,
                                   pipeline_mode=pl.Buffered(1))
    g2 = g_post_l.reshape(1, d_model)
    return pl.pallas_call(
        _out_proj_kernel,
        grid=(bn, s // tm),
        in_specs=[row(d_model),
                  pl.BlockSpec((None, 1, 3 * d_model), lambda b, i: (b, 0, 0)),
                  whole(g2), whole(w_out_l), whole(spread),
                  row(ya.shape[-1]),
                  row(wb), cm(wb), cm(wb),
                  row(LANES), cm(LANES), cm(LANES),
                  row(yc.shape[-1]), row(z.shape[-1])],
        out_specs=row(d_model),
        out_shape=jax.ShapeDtypeStruct(x.shape, x.dtype),
        scratch_shapes=[pltpu.VMEM((wb // LANES, tm, LANES), F32),
                        pltpu.VMEM((wb // LANES, tm, LANES), F32),
                        pltpu.VMEM((1, tm, LANES), F32), pltpu.VMEM((1, tm, LANES), F32)],
        compiler_params=pltpu.CompilerParams(
            dimension_semantics=("parallel", "parallel"), vmem_limit_bytes=VMEM_LIMIT),
        name="out_proj",
    )(x, mod_l.reshape(bn, 1, 3 * d_model), g2, w_out_l, spread, ya, *ob, *lb, yc, z)


def _prepare_w_in(w, sizes, n_kv_a):
    offs = np.concatenate([[0], np.cumsum(sizes)])
    grp = lambda i: w[:, int(offs[i]):int(offs[i + 1])]
    aq, ak, av, bq, bk, bv, cq, ck, cv, z = [grp(i) for i in range(10)]

    def dup(m):
        parts = []
        for g in range(n_kv_a):
            head = m[:, g * HEAD_DIM:(g + 1) * HEAD_DIM]
            parts += [head, head]
        return jnp.concatenate(parts, axis=1)

    groups = [aq * Q_SCALE, dup(ak), dup(av), bq * Q_SCALE, bk, bv, cq * Q_SCALE, cv, z]
    widths = tuple(int(g.shape[1]) for g in groups)
    w_cols = jnp.concatenate(groups, axis=1).astype(BF16)
    return w_cols, ck.T.astype(BF16), widths


def kernel(x, c, rel_table, w_in, w_out, w_ada, b_ada, g_pre, g_post, a_sinks,
           lam_q1, lam_k1, lam_q2, lam_k2, g_sub):
    bn, s, d_model = x.shape
    depth = w_in.shape[0]
    d_mix = w_out.shape[1]
    a_width = b_width = 3 * d_mix // 8
    c_width = d_mix // 4
    n_a_heads = a_width // HEAD_DIM
    n_kv_a = n_a_heads // 4
    n_b_heads = b_width // HEAD_DIM
    n_c_heads = c_width // (2 * HEAD_DIM)
    sizes = (a_width, n_kv_a * HEAD_DIM, n_kv_a * HEAD_DIM, b_width, b_width, b_width,
             c_width, c_width, c_width, d_mix)
    assert w_in.shape[2] == sum(sizes) and rel_table.shape == (REL_BUCKETS,
                                                                n_a_heads + n_b_heads + n_c_heads)
    assert n_a_heads == n_b_heads and s % (N_CLASSES * BLK) == 0
    assert [d for _, d in B_PATTERNS] == [1, BAND_PIECES, N_CLASSES]
    assert s % TILE_M == 0 and TILE_M % BLK == 0

    n_near = -(-(FAR_DIST - 1) // TILE_M) + 1
    bias_a, bias_b, bias_c = _bias_tables(rel_table, n_a_heads, n_b_heads, n_c_heads,
                                          TILE_M, n_near + 1)
    far_bias = rel_table[REL_BUCKETS - 1, n_a_heads + n_b_heads:]
    mod = _modulation(c, w_ada, b_ada)

    head_of_lane = np.arange(b_width) // HEAD_DIM
    spread = jnp.asarray(np.arange(LANES)[:, None] == head_of_lane[None, :], BF16)

    for l in range(depth):
        w_cols, w_kt, widths = _prepare_w_in(w_in[l], sizes, n_kv_a)
        (aq, ak, av, bq, bk, bv, cq, cv, z, bq_cm, bk_cm, bv_cm, ckt) = _in_proj(
            x, mod[l], g_pre[l], w_cols, w_kt, widths, cm_groups=(3, 4, 5))

        ya = _banded_attention(aq, ak, av, bias_a, 2, "natural", sinks=a_sinks[l])
        o1, l1 = _banded_attention(bq, bk, bv, bias_b[0], 1, "natural")
        o4, l4 = _banded_attention(bq_cm, bk_cm, bv_cm, bias_b[1], 1, "pieces")
        o16, l16 = _banded_attention(bq_cm, bk_cm, bv_cm, bias_b[2], 1, "classes")
        ob, lb = (o1, o4, o16), (l1, l4, l16)

        lam_init = 0.8 - 0.6 * math.exp(-0.3 * l)
        lam_params = jnp.stack([lam_q1[l], lam_k1[l], lam_q2[l], lam_k2[l]]).astype(F32)
        yc = _diff_attention(cq, ckt, cv, bias_c, far_bias, lam_params, g_sub[l], lam_init)

        x = _out_proj(x, mod[l], g_post[l], w_out[l].astype(BF16), spread, ya, ob, lb, yc, z)
    return x
```

```python
import functools
import math

import numpy as np
import jax
import jax.numpy as jnp
from jax import lax
from jax.experimental import pallas as pl
from jax.experimental.pallas import tpu as pltpu

F32 = jnp.float32
BF16 = jnp.bfloat16

HEAD_DIM = 64
LANES = 128
BLK = 128
A_WINDOW = 128
B_PATTERNS = ((128, 1), (512, 4), (2048, 16))
REL_BUCKETS = 32
REL_MAX_DIST = 2048
EPS = 1e-6
NEG = -1e30
LOG2E = 1.4426950408889634
Q_SCALE = LOG2E / math.sqrt(HEAD_DIM)

TILE_M = 512
BAND_BLOCKS = 8
N_CLASSES = 16
BAND_PIECES = 4
PIECED_PATTERN = 1
DIFF_ROWS = 128
DIFF_MAX_TILES = 4
VMEM_LIMIT = 56 * 1024 * 1024


def _class_residue(c):
    return c // BAND_PIECES + (N_CLASSES // BAND_PIECES) * (c % BAND_PIECES)


def _rel_bucket_np(n):
    n = np.maximum(np.asarray(n, np.int64), 0)
    max_exact = REL_BUCKETS // 2
    nf = np.maximum(n, 1).astype(np.float32)
    large = max_exact + (np.log(nf / np.float32(max_exact))
                         / np.float32(math.log(REL_MAX_DIST / max_exact))
                         * np.float32(REL_BUCKETS - max_exact)).astype(np.int32)
    large = np.minimum(large, REL_BUCKETS - 1)
    return np.where(n < max_exact, n, large)


_BUCKET_OF = _rel_bucket_np(np.arange(0, 1 << 18))
_THR = [0] + [int(np.argmax(_BUCKET_OF >= b)) for b in range(1, REL_BUCKETS)]
FAR_DIST = _THR[REL_BUCKETS - 1]


def _bucket_values(dist, dist_scale, lo, hi, table_ref, col):
    b_lo = int(_BUCKET_OF[max(lo, 0) * dist_scale])
    b_hi = int(_BUCKET_OF[max(hi, 0) * dist_scale])
    val = jnp.full(dist.shape, table_ref[b_lo, col] * LOG2E, F32)
    for b in range(b_lo + 1, b_hi + 1):
        thr = -(-_THR[b] // dist_scale)
        val = jnp.where(dist >= thr, table_ref[b, col] * LOG2E, val)
    return val


def _band_bias_kernel(table_ref, a_ref, b_ref):
    h = pl.program_id(0)
    n_a_heads = pl.num_programs(0)
    row = lax.broadcasted_iota(jnp.int32, (BLK, 2 * BLK), 0)
    col = lax.broadcasted_iota(jnp.int32, (BLK, 2 * BLK), 1)
    first_ok = col >= BLK

    def pieced(idx):
        per = BLK // BAND_PIECES
        return (idx & (per - 1)) * BAND_PIECES + (idx >> (per.bit_length() - 1))

    def tile(dist, max_dist, dist_scale, table_col):
        val = _bucket_values(dist, dist_scale, 0, max_dist, table_ref, table_col)
        val = jnp.where((dist >= 0) & (dist <= max_dist), val, NEG)
        return jnp.where(first_ok, val, NEG), val

    dist = row + BLK - col
    a_ref[0], a_ref[1] = tile(dist, A_WINDOW - 1, 1, h)
    for p, (w, d) in enumerate(B_PATTERNS):
        if p == PIECED_PATTERN:
            within = col & (BLK - 1)
            dist_p = pieced(row) + BLK - (pieced(within) + (col - within))
        else:
            dist_p = dist
        b_ref[p, 0], b_ref[p, 1] = tile(dist_p, w // d, d, n_a_heads + h)


def _diff_bias_kernel(table_ref, out_ref, *, tile, n_tiles, head_offset):
    h = pl.program_id(0)
    row = lax.broadcasted_iota(jnp.int32, (BLK, BLK), 0)
    col = lax.broadcasted_iota(jnp.int32, (BLK, BLK), 1)
    sub = tile // BLK
    pieces = {}
    for delta in range(-(sub - 1), n_tiles * sub):
        if delta < 0:
            pieces[delta] = jnp.full((BLK, BLK), NEG, F32)
            continue
        dist = row - col + delta * BLK
        val = _bucket_values(dist, 1, delta * BLK - (BLK - 1), delta * BLK + BLK - 1,
                             table_ref, head_offset + h)
        pieces[delta] = jnp.where(dist >= 0, val, NEG) if delta == 0 else val
    for dt in range(n_tiles):
        for a in range(sub):
            for b in range(sub):
                out_ref[dt, a * BLK:(a + 1) * BLK, b * BLK:(b + 1) * BLK] = pieces[dt * sub + a - b]


def _bias_tables(rel_table, n_a_heads, n_b_heads, n_c_heads, tile, n_tiles):
    smem = pl.BlockSpec(memory_space=pltpu.SMEM)
    bias_a, bias_b = pl.pallas_call(
        _band_bias_kernel,
        grid=(n_a_heads,),
        in_specs=[smem],
        out_specs=[pl.BlockSpec((2, BLK, 2 * BLK), lambda h: (0, h, 0)),
                   pl.BlockSpec((3, 2, BLK, 2 * BLK), lambda h: (0, 0, h, 0))],
        out_shape=[jax.ShapeDtypeStruct((2, n_a_heads * BLK, 2 * BLK), F32),
                   jax.ShapeDtypeStruct((3, 2, n_b_heads * BLK, 2 * BLK), F32)],
        name="band_bias",
    )(rel_table)
    bias_c = pl.pallas_call(
        functools.partial(_diff_bias_kernel, tile=tile, n_tiles=n_tiles,
                          head_offset=n_a_heads + n_b_heads),
        grid=(n_c_heads,),
        in_specs=[smem],
        out_specs=pl.BlockSpec((None, n_tiles, tile, tile), lambda h: (h, 0, 0, 0)),
        out_shape=jax.ShapeDtypeStruct((n_c_heads, n_tiles, tile, tile), F32),
        compiler_params=pltpu.CompilerParams(vmem_limit_bytes=VMEM_LIMIT),
        name="diff_bias",
    )(rel_table)
    return bias_a, bias_b, bias_c


def _split_bf16(v):
    hi = v.astype(BF16)
    lo = (v - hi.astype(F32)).astype(BF16)
    return hi, lo


def _mod_kernel(c_ref, w_ref, b_ref, out_ref):
    c = c_ref[...]
    s = c / (1.0 + jnp.exp(-c))
    s_hi, s_lo = _split_bf16(s)
    w_hi, w_lo = _split_bf16(w_ref[...])
    acc = jnp.dot(s_hi, w_hi, preferred_element_type=F32)
    acc += jnp.dot(s_lo, w_hi, preferred_element_type=F32)
    acc += jnp.dot(s_hi, w_lo, preferred_element_type=F32)
    out_ref[...] = acc + b_ref[...]


def _modulation(c, w_ada, b_ada):
    depth, d_model, n_out = w_ada.shape
    rows = 8
    c_pad = jnp.zeros((rows, d_model), F32).at[:c.shape[0]].set(c)
    tn = 768
    out = pl.pallas_call(
        _mod_kernel,
        grid=(depth, n_out // tn),
        in_specs=[pl.BlockSpec((rows, d_model), lambda l, j: (0, 0)),
                  pl.BlockSpec((None, d_model, tn), lambda l, j: (l, 0, j)),
                  pl.BlockSpec((None, 1, tn), lambda l, j: (l, 0, j))],
        out_specs=pl.BlockSpec((None, rows, tn), lambda l, j: (l, 0, j)),
        out_shape=jax.ShapeDtypeStruct((depth, rows, n_out), F32),
        name="adaln_mod",
    )(c_pad, w_ada, b_ada.reshape(depth, 1, n_out))
    return out[:, :c.shape[0]]


def _in_proj_kernel(x_ref, mod_ref, g_ref, w_ref, wkt_ref, perm_ref, *refs, widths, cm_groups):
    n_out = len(widths)
    col_refs = refs[:n_out]
    cm_refs = refs[n_out:n_out + len(cm_groups)]
    kt_ref = refs[n_out + len(cm_groups)]
    d_model = x_ref.shape[-1]
    x = x_ref[...]
    ms = jnp.mean(x * x, axis=-1, keepdims=True)
    shift = mod_ref[:, :d_model]
    scale = mod_ref[:, d_model:2 * d_model]
    h = (x * lax.rsqrt(ms + EPS)) * g_ref[...]
    h = (h * (1.0 + scale) + shift).astype(BF16)
    off = 0
    for gi, (ref, width) in enumerate(zip(col_refs, widths)):
        res = jnp.dot(h, w_ref[:, off:off + width],
                      preferred_element_type=F32).astype(ref.dtype)
        ref[...] = res
        if gi in cm_groups:
            cm_ref = cm_refs[cm_groups.index(gi)]
            n_cls, per_cls = cm_ref.shape[0], cm_ref.shape[1]
            perm = jnp.dot(perm_ref[...], res, preferred_element_type=F32).astype(cm_ref.dtype)
            for c in range(n_cls):
                cm_ref[c] = perm[c * per_cls:(c + 1) * per_cls]
        off += width
    kt_ref[...] = lax.dot_general(wkt_ref[...], h, (((1,), (1,)), ((), ())),
                                  preferred_element_type=F32).astype(kt_ref.dtype)


def _class_major_permutation(tm):
    per_cls = tm // N_CLASSES
    src = np.array([N_CLASSES * j + _class_residue(c)
                    for c in range(N_CLASSES) for j in range(per_cls)])
    return jnp.asarray(np.arange(tm)[None, :] == src[:, None], BF16)


def _in_proj(x, mod_l, g_pre_l, w_cols, w_kt, widths, cm_groups):
    bn, s, d_model = x.shape
    tm = TILE_M
    n_t = s // tm
    n_cols = w_cols.shape[1]
    kt_rows = w_kt.shape[0]
    per_cls = tm // N_CLASSES
    out_shape = [jax.ShapeDtypeStruct((bn, s, w), BF16) for w in widths]
    out_specs = [pl.BlockSpec((None, tm, w), lambda b, i: (b, i, 0)) for w in widths]
    for gi in cm_groups:
        out_shape.append(jax.ShapeDtypeStruct((bn, N_CLASSES, s // N_CLASSES, widths[gi]), BF16))
        out_specs.append(pl.BlockSpec((None, N_CLASSES, per_cls, widths[gi]),
                                      lambda b, i: (b, 0, i, 0)))
    out_shape.append(jax.ShapeDtypeStruct((bn, n_t, kt_rows, tm), BF16))
    out_specs.append(pl.BlockSpec((None, None, kt_rows, tm), lambda b, i: (b, i, 0, 0)))
    return pl.pallas_call(
        functools.partial(_in_proj_kernel, widths=widths, cm_groups=cm_groups),
        grid=(bn, n_t),
        in_specs=[pl.BlockSpec((None, tm, d_model), lambda b, i: (b, i, 0)),
                  pl.BlockSpec((None, 1, 3 * d_model), lambda b, i: (b, 0, 0)),
                  pl.BlockSpec((1, d_model), lambda b, i: (0, 0)),
                  pl.BlockSpec((d_model, n_cols), lambda b, i: (0, 0),
                               pipeline_mode=pl.Buffered(1)),
                  pl.BlockSpec((kt_rows, d_model), lambda b, i: (0, 0),
                               pipeline_mode=pl.Buffered(1)),
                  pl.BlockSpec((tm, tm), lambda b, i: (0, 0), pipeline_mode=pl.Buffered(1))],
        out_specs=out_specs,
        out_shape=out_shape,
        compiler_params=pltpu.CompilerParams(
            dimension_semantics=("parallel", "parallel"), vmem_limit_bytes=VMEM_LIMIT),
        name="in_proj",
    )(x, mod_l.reshape(bn, 1, 3 * d_model), g_pre_l.reshape(1, d_model), w_cols, w_kt,
      _class_major_permutation(tm))


def _lane_masks():
    lane = lax.broadcasted_iota(jnp.int32, (1, LANES), 1)
    even = lane < HEAD_DIM
    m_even = jnp.where(even, 1.0, 0.0).astype(BF16)
    m_odd = jnp.where(even, 0.0, 1.0).astype(BF16)
    return even, m_even, m_odd


def _banded_kernel(*refs, n_pairs, pairs_per_kv, has_sink, pieces):
    it = iter(refs)
    sink_ref = next(it) if has_sink else None
    q_ref, kp_ref, kc_ref, vp_ref, vc_ref, bias_ref, o_ref = (next(it) for _ in range(7))
    lse_ref = None if has_sink else next(it)
    k_buf, v_buf, s_buf = next(it), next(it), next(it)
    step = pl.program_id(2)
    if pieces == 1:
        nblk = q_ref.shape[0] // BLK
        k_buf[:BLK] = kp_ref[...]
        k_buf[BLK:] = kc_ref[...]
        v_buf[:BLK] = vp_ref[...]
        v_buf[BLK:] = vc_ref[...]
        q_src, o_dst, lse_dst = q_ref, o_ref, lse_ref
    else:
        q_src, o_dst, lse_dst = next(it), next(it), next(it)
        per = BLK // pieces
        nblk = q_ref.shape[1] // per
        for pc in range(pieces):
            k_buf[pc * per:(pc + 1) * per] = kp_ref[pc]
            v_buf[pc * per:(pc + 1) * per] = vp_ref[pc]
            for ib in range(nblk):
                dst = (ib * pieces + pc) * per
                src = slice(ib * per, (ib + 1) * per)
                q_src[dst:dst + per] = q_ref[pc, src]
                k_buf[BLK + dst:BLK + dst + per] = kc_ref[pc, src]
                v_buf[BLK + dst:BLK + dst + per] = vc_ref[pc, src]
    even, m_even, m_odd = _lane_masks()
    lane = lax.broadcasted_iota(jnp.int32, (BLK, LANES), 1)
    top_rows = lax.broadcasted_iota(jnp.int32, (2 * BLK, 1), 0) < BLK
    ones = jnp.ones((2 * BLK, LANES), BF16)

    for ib in range(nblk):
        r0 = ib * BLK
        variant = jnp.where(step == 0, 0, 1) if ib == 0 else 1
        lse_tile = jnp.ones((BLK, LANES), F32)
        for j in range(n_pairs):
            g = j // pairs_per_kv
            q2 = q_src[pl.ds(r0, BLK), j * LANES:(j + 1) * LANES]
            qs = jnp.concatenate([q2 * m_even, q2 * m_odd], axis=0)
            k2 = k_buf[pl.ds(r0, 2 * BLK), g * LANES:(g + 1) * LANES]
            s = lax.dot_general(qs, k2, (((1,), (1,)), ((), ())), preferred_element_type=F32)
            s_buf[ib * n_pairs + j] = s + bias_ref[variant, j * 2 * BLK:(j + 1) * 2 * BLK, :]
        for j in range(n_pairs):
            g = j // pairs_per_kv
            v2 = v_buf[pl.ds(r0, 2 * BLK), g * LANES:(g + 1) * LANES]
            s = s_buf[ib * n_pairs + j]
            m = jnp.max(s, axis=-1, keepdims=True)
            if has_sink:
                sk = jnp.where(top_rows, sink_ref[2 * j], sink_ref[2 * j + 1]) * LOG2E
                m = jnp.maximum(m, sk)
            e = jnp.exp2(s - m)
            ox = jnp.dot(e.astype(BF16), jnp.concatenate([v2, ones], axis=1),
                         preferred_element_type=F32)
            ov, den = ox[:, :LANES], ox[:, LANES:]
            if has_sink:
                ov = ov / (den + jnp.exp2(sk - m))
            o_dst[pl.ds(r0, BLK), j * LANES:(j + 1) * LANES] = (
                jnp.where(even, ov[:BLK], ov[BLK:]).astype(o_dst.dtype))
            if lse_dst is not None:
                for hh, half in ((2 * j, slice(0, BLK)), (2 * j + 1, slice(BLK, 2 * BLK))):
                    lse_tile = jnp.where(lane == hh, m[half], lse_tile)
                    lse_tile = jnp.where(lane == HEAD_DIM + hh, den[half], lse_tile)
        if lse_dst is not None:
            lse_dst[pl.ds(r0, BLK), :] = lse_tile

    if pieces > 1:
        for pc in range(pieces):
            for ib in range(nblk):
                dst = (ib * pieces + pc) * per
                o_ref[pc, ib * per:(ib + 1) * per] = o_dst[dst:dst + per]
                lse_ref[pc, ib * per:(ib + 1) * per] = lse_dst[dst:dst + per]


def _banded_attention(q, k, v, bias, pairs_per_kv, layout, sinks=None):
    wq, wkv = q.shape[-1], k.shape[-1]
    bn = q.shape[0]
    n_pairs = wq // LANES
    has_sink = sinks is not None
    if layout == "natural":
        seq = q.shape[1]
        nblk = min(BAND_BLOCKS, seq // BLK)
        rows = nblk * BLK
        grid = (bn, 1, seq // rows)
        cur = lambda b, r, n: (b, n, 0)
        prev = lambda b, r, n: (b, jnp.maximum(n * nblk - 1, 0), 0)
        cur_blk = lambda w: (None, rows, w)
        prev_blk = lambda w: (None, BLK, w)
        pieces = 1
    elif layout == "classes":
        n_cls, seq = q.shape[1], q.shape[2]
        nblk = min(BAND_BLOCKS, seq // BLK)
        rows = nblk * BLK
        grid = (bn, n_cls, seq // rows)
        cur = lambda b, r, n: (b, r, n, 0)
        prev = lambda b, r, n: (b, r, jnp.maximum(n * nblk - 1, 0), 0)
        cur_blk = lambda w: (None, None, rows, w)
        prev_blk = lambda w: (None, None, BLK, w)
        pieces = 1
    else:
        n_cls, seq = q.shape[1], q.shape[2]
        pieces = BAND_PIECES
        per = BLK // pieces
        nblk = pieces
        rows = BLK
        grid = (bn, n_cls // pieces, seq // rows)
        cur = lambda b, r, n: (b, r, n, 0)
        prev = lambda b, r, n: (b, r, jnp.maximum(n * (rows // per) - 1, 0), 0)
        cur_blk = lambda w: (None, pieces, rows, w)
        prev_blk = lambda w: (None, pieces, per, w)
    tot = nblk * BLK
    in_specs = [pl.BlockSpec(cur_blk(wq), cur),
                pl.BlockSpec(prev_blk(wkv), prev),
                pl.BlockSpec(cur_blk(wkv), cur),
                pl.BlockSpec(prev_blk(wkv), prev),
                pl.BlockSpec(cur_blk(wkv), cur),
                pl.BlockSpec(bias.shape, lambda b, r, n: (0, 0, 0))]
    args = [q, k, k, v, v, bias]
    out_specs = [pl.BlockSpec(cur_blk(wq), cur)]
    out_shape = [jax.ShapeDtypeStruct(q.shape, BF16)]
    scratch = [pltpu.VMEM((tot + BLK, wkv), BF16), pltpu.VMEM((tot + BLK, wkv), BF16),
               pltpu.VMEM((nblk * n_pairs, 2 * BLK, 2 * BLK), F32)]
    if has_sink:
        in_specs.insert(0, pl.BlockSpec(memory_space=pltpu.SMEM))
        args.insert(0, sinks)
    else:
        out_specs.append(pl.BlockSpec(cur_blk(LANES), cur))
        out_shape.append(jax.ShapeDtypeStruct(q.shape[:-1] + (LANES,), F32))
    if pieces > 1:
        scratch += [pltpu.VMEM((tot, wq), BF16), pltpu.VMEM((tot, wq), BF16),
                    pltpu.VMEM((tot, LANES), F32)]
    outs = pl.pallas_call(
        functools.partial(_banded_kernel, n_pairs=n_pairs, pairs_per_kv=pairs_per_kv,
                          has_sink=has_sink, pieces=pieces),
        grid=grid,
        in_specs=in_specs,
        out_specs=out_specs,
        out_shape=out_shape,
        scratch_shapes=scratch,
        compiler_params=pltpu.CompilerParams(
            dimension_semantics=("parallel", "parallel", "arbitrary"),
            vmem_limit_bytes=VMEM_LIMIT),
        name="banded_sink" if has_sink else f"banded_{layout}",
    )(*args)
    return outs[0] if has_sink else tuple(outs)


def _diff_kernel(far_ref, q_ref, kt_ref, v_ref, bias_ref, lam_ref, g_ref, o_ref,
                 qs_s, vx_s, m_s, acc_s, s_s, *, lam_init, n_near):
    h = pl.program_id(1)
    i = pl.program_id(2)
    t = q_ref.shape[0]
    n_rep = t // LANES

    @pl.when(i == 0)
    def _():
        vx_s[:, :LANES] = v_ref[...]
        vx_s[:, LANES:] = jnp.ones((vx_s.shape[0], LANES), BF16)

    _, m_even, m_odd = _lane_masks()
    q2 = q_ref[...]
    qs_s[:t] = q2 * m_even
    qs_s[t:] = q2 * m_odd
    m_s[...] = jnp.full(m_s.shape, NEG, F32)
    acc_s[...] = jnp.zeros(acc_s.shape, F32)
    far_bias = far_ref[h] * LOG2E

    def sweep(j0, n_tiles, near):
        k0 = pl.multiple_of(j0 * t, t)
        kts = [kt_ref[j0 + a] for a in range(n_tiles)]
        vx = vx_s[pl.ds(k0, n_tiles * t), :]
        width = n_tiles * t
        for r0 in range(0, 2 * t, DIFF_ROWS):
            rows = pl.ds(r0, DIFF_ROWS)
            qc = qs_s[rows, :]
            for a, kt in enumerate(kts):
                part = jnp.dot(qc, kt, preferred_element_type=F32)
                if near:
                    part = part + bias_ref[jnp.minimum(i - j0 - a, n_near),
                                           pl.ds(r0 % t, DIFF_ROWS), :]
                s_s[rows, a * t:(a + 1) * t] = part
        shift = 0.0 if near else far_bias
        for r0 in range(0, 2 * t, DIFF_ROWS):
            rows = pl.ds(r0, DIFF_ROWS)
            s = s_s[rows, :width]
            m_cur = jnp.max(s, axis=-1, keepdims=True) + shift
            m_old = m_s[rows, :]
            m_new = jnp.maximum(m_old, m_cur)
            alpha = jnp.exp2(m_old - m_new)
            m_sub = m_new - shift
            p = jnp.exp2(s - jnp.concatenate([m_sub] * (n_tiles * n_rep), axis=1))
            pv = jnp.dot(p.astype(BF16), vx, preferred_element_type=F32)
            acc_s[rows, :] = jnp.concatenate([alpha, alpha], axis=1) * acc_s[rows, :] + pv
            m_s[rows, :] = m_new

    n_far = jnp.maximum(i - (n_near - 1), 0)
    n_quads = n_far // DIFF_MAX_TILES
    j_pairs = DIFF_MAX_TILES * n_quads
    n_far_pairs = (n_far - j_pairs) >> 1
    j_near0 = j_pairs + 2 * n_far_pairs
    n_near_quads = jnp.where(i + 1 - j_near0 >= DIFF_MAX_TILES, 1, 0)
    j_near = j_near0 + DIFF_MAX_TILES * n_near_quads
    odd = (i + 1 - j_near) & 1

    def far_quad(jj, carry):
        sweep(DIFF_MAX_TILES * jj, DIFF_MAX_TILES, False)
        return carry

    def far_pair(jj, carry):
        sweep(j_pairs + 2 * jj, 2, False)
        return carry

    def near_quad(jj, carry):
        sweep(j_near0, DIFF_MAX_TILES, True)
        return carry

    lax.fori_loop(0, n_quads, far_quad, 0)
    lax.fori_loop(0, n_far_pairs, far_pair, 0)
    lax.fori_loop(0, n_near_quads, near_quad, 0)

    @pl.when(odd == 1)
    def _():
        sweep(j_near, 1, True)

    def near_pair(kk, carry):
        sweep(j_near + odd + 2 * kk, 2, True)
        return carry

    lax.fori_loop(0, (i + 1 - j_near) >> 1, near_pair, 0)

    acc = acc_s[...]
    o = acc[:, :LANES] / acc[:, LANES:]
    lam = (jnp.exp(jnp.sum(lam_ref[0:1, :] * lam_ref[1:2, :], axis=-1, keepdims=True))
           - jnp.exp(jnp.sum(lam_ref[2:3, :] * lam_ref[3:4, :], axis=-1, keepdims=True)) + lam_init)
    y = o[:t] - lam * o[t:]
    y = y * lax.rsqrt(jnp.mean(y * y, axis=-1, keepdims=True) + EPS)
    o_ref[...] = (y * g_ref[...] * (1.0 - lam_init)).astype(o_ref.dtype)


def _diff_attention(q, kt, v, bias_c, far_bias, lam_params, g_sub_l, lam_init):
    bn, s, w = q.shape
    t = TILE_M
    n_t = s // t
    n_heads = w // LANES
    n_near = bias_c.shape[1] - 1
    return pl.pallas_call(
        functools.partial(_diff_kernel, lam_init=lam_init, n_near=n_near),
        grid=(bn, n_heads, n_t),
        in_specs=[pl.BlockSpec(memory_space=pltpu.SMEM),
                  pl.BlockSpec((None, t, LANES), lambda b, h, i: (b, i, h)),
                  pl.BlockSpec((None, n_t, LANES, t), lambda b, h, i: (b, 0, h, 0)),
                  pl.BlockSpec((None, s, LANES), lambda b, h, i: (b, 0, h)),
                  pl.BlockSpec((None, n_near + 1, t, t), lambda b, h, i: (h, 0, 0, 0)),
                  pl.BlockSpec((4, HEAD_DIM), lambda b, h, i: (0, 0)),
                  pl.BlockSpec((1, LANES), lambda b, h, i: (0, 0))],
        out_specs=pl.BlockSpec((None, t, LANES), lambda b, h, i: (b, i, h)),
        out_shape=jax.ShapeDtypeStruct((bn, s, w), BF16),
        scratch_shapes=[pltpu.VMEM((2 * t, LANES), BF16),
                        pltpu.VMEM((s, 2 * LANES), BF16),
                        pltpu.VMEM((2 * t, LANES), F32),
                        pltpu.VMEM((2 * t, 2 * LANES), F32),
                        pltpu.VMEM((2 * t, DIFF_MAX_TILES * t), F32)],
        compiler_params=pltpu.CompilerParams(
            dimension_semantics=("parallel", "parallel", "arbitrary"),
            vmem_limit_bytes=VMEM_LIMIT),
        name="diff_attn",
    )(far_bias, q, kt, v, bias_c, lam_params, g_sub_l.reshape(1, LANES))


def _out_proj_kernel(x_ref, mod_ref, g_ref, w_ref, e_ref, unperm_ref, ya_ref, o1_ref, o4_ref,
                     o16_ref, l1_ref, l4_ref, l16_ref, yc_ref, z_ref, out_ref, l4_scr, l16_scr):
    d_model = x_ref.shape[-1]
    wa = ya_ref.shape[-1]
    wb = o1_ref.shape[-1]
    z = z_ref[...].astype(F32)
    sz = z / (1.0 + jnp.exp(-z))

    def to_natural(cm_ref, scr):
        n_cls, per_cls = cm_ref.shape[0], cm_ref.shape[1]
        for k in range(scr.shape[0]):
            for c in range(n_cls):
                scr[k, pl.ds(_class_residue(c), per_cls, stride=n_cls), :] = cm_ref[
                    c, :, k * LANES:(k + 1) * LANES].astype(F32)
        return jnp.concatenate([scr[k] for k in range(scr.shape[0])], axis=1)

    def rows_to_natural(cm_ref):
        n_cls, per_cls, w = cm_ref.shape
        return jnp.dot(unperm_ref[...], cm_ref[...].reshape(n_cls * per_cls, w),
                       preferred_element_type=F32)

    o_parts = [o1_ref[...].astype(F32), rows_to_natural(o4_ref), rows_to_natural(o16_ref)]
    stats = [l1_ref[...], to_natural(l4_ref, l4_scr), to_natural(l16_ref, l16_scr)]
    head_lane = lax.broadcasted_iota(jnp.int32, stats[0].shape, 1) < wb // HEAD_DIM
    maxes = [jnp.where(head_lane, st, 0.0) for st in stats]
    dens = [jnp.where(head_lane, pltpu.roll(st, HEAD_DIM, 1), 1.0) for st in stats]
    lses = [mxp + jnp.log2(dn) for mxp, dn in zip(maxes, dens)]
    mx = jnp.maximum(jnp.maximum(lses[0], lses[1]), lses[2])
    ws = [jnp.exp2(l - mx) for l in lses]
    tot = ws[0] + ws[1] + ws[2]
    yb = jnp.zeros((x_ref.shape[0], wb), F32)
    for wgt, dn, o_part in zip(ws, dens, o_parts):
        a_hi, a_lo = _split_bf16(wgt / (tot * dn))
        spread = (jnp.dot(a_hi, e_ref[...], preferred_element_type=F32)
                  + jnp.dot(a_lo, e_ref[...], preferred_element_type=F32))
        yb = yb + spread * o_part

    ga = (ya_ref[...].astype(F32) * sz[:, :wa]).astype(BF16)
    gb = (yb * sz[:, wa:wa + wb]).astype(BF16)
    gc = (yc_ref[...].astype(F32) * sz[:, wa + wb:]).astype(BF16)
    y = jnp.dot(ga, w_ref[:wa], preferred_element_type=F32)
    y += jnp.dot(gb, w_ref[wa:wa + wb], preferred_element_type=F32)
    y += jnp.dot(gc, w_ref[wa + wb:], preferred_element_type=F32)
    r = (y * lax.rsqrt(jnp.mean(y * y, axis=-1, keepdims=True) + EPS)) * g_ref[...]
    gate = mod_ref[:, 2 * d_model:]
    out_ref[...] = x_ref[...] + gate * r


def _out_proj(x, mod_l, g_post_l, w_out_l, spread, ya, ob, lb, yc, z):
    bn, s, d_model = x.shape
    tm = TILE_M
    per_cls = tm // N_CLASSES
    wb = ob[0].shape[-1]
    row = lambda w: pl.BlockSpec((None, tm, w), lambda b, i: (b, i, 0))
    cm = lambda w: pl.BlockSpec((None, N_CLASSES, per_cls, w), lambda b, i: (b, 0, i, 0))
    whole = lambda a: pl.BlockSpec(a.shape, lambda b, i: (0,) * a.ndim,
                                   pipeline_mode=pl.Buffered(1))
    g2 = g_post_l.reshape(1, d_model)
    unperm = _class_major_permutation(tm).T
    return pl.pallas_call(
        _out_proj_kernel,
        grid=(bn, s // tm),
        in_specs=[row(d_model),
                  pl.BlockSpec((None, 1, 3 * d_model), lambda b, i: (b, 0, 0)),
                  whole(g2), whole(w_out_l), whole(spread), whole(unperm),
                  row(ya.shape[-1]),
                  row(wb), cm(wb), cm(wb),
                  row(LANES), cm(LANES), cm(LANES),
                  row(yc.shape[-1]), row(z.shape[-1])],
        out_specs=row(d_model),
        out_shape=jax.ShapeDtypeStruct(x.shape, x.dtype),
        scratch_shapes=[pltpu.VMEM((1, tm, LANES), F32), pltpu.VMEM((1, tm, LANES), F32)],
        compiler_params=pltpu.CompilerParams(
            dimension_semantics=("parallel", "parallel"), vmem_limit_bytes=VMEM_LIMIT),
        name="out_proj",
    )(x, mod_l.reshape(bn, 1, 3 * d_model), g2, w_out_l, spread, unperm, ya, *ob, *lb, yc, z)


def _prepare_w_in(w, sizes, n_kv_a):
    offs = np.concatenate([[0], np.cumsum(sizes)])
    grp = lambda i: w[:, int(offs[i]):int(offs[i + 1])]
    aq, ak, av, bq, bk, bv, cq, ck, cv, z = [grp(i) for i in range(10)]

    def dup(m):
        parts = []
        for g in range(n_kv_a):
            head = m[:, g * HEAD_DIM:(g + 1) * HEAD_DIM]
            parts += [head, head]
        return jnp.concatenate(parts, axis=1)

    groups = [aq * Q_SCALE, dup(ak), dup(av), bq * Q_SCALE, bk, bv, cq * Q_SCALE, cv, z]
    widths = tuple(int(g.shape[1]) for g in groups)
    w_cols = jnp.concatenate(groups, axis=1).astype(BF16)
    return w_cols, ck.T.astype(BF16), widths


def kernel(x, c, rel_table, w_in, w_out, w_ada, b_ada, g_pre, g_post, a_sinks,
           lam_q1, lam_k1, lam_q2, lam_k2, g_sub):
    bn, s, d_model = x.shape
    depth = w_in.shape[0]
    d_mix = w_out.shape[1]
    a_width = b_width = 3 * d_mix // 8
    c_width = d_mix // 4
    n_a_heads = a_width // HEAD_DIM
    n_kv_a = n_a_heads // 4
    n_b_heads = b_width // HEAD_DIM
    n_c_heads = c_width // (2 * HEAD_DIM)
    sizes = (a_width, n_kv_a * HEAD_DIM, n_kv_a * HEAD_DIM, b_width, b_width, b_width,
             c_width, c_width, c_width, d_mix)
    assert w_in.shape[2] == sum(sizes) and rel_table.shape == (REL_BUCKETS,
                                                                n_a_heads + n_b_heads + n_c_heads)
    assert n_a_heads == n_b_heads and s % (N_CLASSES * BLK) == 0
    assert [d for _, d in B_PATTERNS] == [1, BAND_PIECES, N_CLASSES]
    assert s % TILE_M == 0 and TILE_M % BLK == 0

    n_near = -(-(FAR_DIST - 1) // TILE_M) + 1
    bias_a, bias_b, bias_c = _bias_tables(rel_table, n_a_heads, n_b_heads, n_c_heads,
                                          TILE_M, n_near + 1)
    far_bias = rel_table[REL_BUCKETS - 1, n_a_heads + n_b_heads:]
    mod = _modulation(c, w_ada, b_ada)

    head_of_lane = np.arange(b_width) // HEAD_DIM
    spread = jnp.asarray(np.arange(LANES)[:, None] == head_of_lane[None, :], BF16)

    for l in range(depth):
        w_cols, w_kt, widths = _prepare_w_in(w_in[l], sizes, n_kv_a)
        (aq, ak, av, bq, bk, bv, cq, cv, z, bq_cm, bk_cm, bv_cm, ckt) = _in_proj(
            x, mod[l], g_pre[l], w_cols, w_kt, widths, cm_groups=(3, 4, 5))

        ya = _banded_attention(aq, ak, av, bias_a, 2, "natural", sinks=a_sinks[l])
        o1, l1 = _banded_attention(bq, bk, bv, bias_b[0], 1, "natural")
        o4, l4 = _banded_attention(bq_cm, bk_cm, bv_cm, bias_b[1], 1, "pieces")
        o16, l16 = _banded_attention(bq_cm, bk_cm, bv_cm, bias_b[2], 1, "classes")
        ob, lb = (o1, o4, o16), (l1, l4, l16)

        lam_init = 0.8 - 0.6 * math.exp(-0.3 * l)
        lam_params = jnp.stack([lam_q1[l], lam_k1[l], lam_q2[l], lam_k2[l]]).astype(F32)
        yc = _diff_attention(cq, ckt, cv, bias_c, far_bias, lam_params, g_sub[l], lam_init)

        x = _out_proj(x, mod[l], g_post[l], w_out[l].astype(BF16), spread, ya, ob, lb, yc, z)
    return x
```

```python
import functools
import math

import numpy as np
import jax
import jax.numpy as jnp
from jax import lax
from jax.experimental import pallas as pl
from jax.experimental.pallas import tpu as pltpu

F32 = jnp.float32
BF16 = jnp.bfloat16

HEAD_DIM = 64
LANES = 128
BLK = 128
A_WINDOW = 128
B_PATTERNS = ((128, 1), (512, 4), (2048, 16))
REL_BUCKETS = 32
REL_MAX_DIST = 2048
EPS = 1e-6
NEG = -1e30
LOG2E = 1.4426950408889634
Q_SCALE = LOG2E / math.sqrt(HEAD_DIM)

TILE_M = 512
BAND_BLOCKS = 8
N_CLASSES = 16
BAND_PIECES = 4
PIECED_PATTERN = 1
DIFF_ROWS = 128
DIFF_MAX_TILES = 4
VMEM_LIMIT = 56 * 1024 * 1024


def _class_residue(c):
    return c // BAND_PIECES + (N_CLASSES // BAND_PIECES) * (c % BAND_PIECES)


def _rel_bucket_np(n):
    n = np.maximum(np.asarray(n, np.int64), 0)
    max_exact = REL_BUCKETS // 2
    nf = np.maximum(n, 1).astype(np.float32)
    large = max_exact + (np.log(nf / np.float32(max_exact))
                         / np.float32(math.log(REL_MAX_DIST / max_exact))
                         * np.float32(REL_BUCKETS - max_exact)).astype(np.int32)
    large = np.minimum(large, REL_BUCKETS - 1)
    return np.where(n < max_exact, n, large)


_BUCKET_OF = _rel_bucket_np(np.arange(0, 1 << 18))
_THR = [0] + [int(np.argmax(_BUCKET_OF >= b)) for b in range(1, REL_BUCKETS)]
FAR_DIST = _THR[REL_BUCKETS - 1]


def _bucket_values(dist, dist_scale, lo, hi, table_ref, col):
    b_lo = int(_BUCKET_OF[max(lo, 0) * dist_scale])
    b_hi = int(_BUCKET_OF[max(hi, 0) * dist_scale])
    val = jnp.full(dist.shape, table_ref[b_lo, col] * LOG2E, F32)
    for b in range(b_lo + 1, b_hi + 1):
        thr = -(-_THR[b] // dist_scale)
        val = jnp.where(dist >= thr, table_ref[b, col] * LOG2E, val)
    return val


def _band_bias_kernel(table_ref, a_ref, b_ref):
    h = pl.program_id(0)
    n_a_heads = pl.num_programs(0)
    row = lax.broadcasted_iota(jnp.int32, (BLK, 2 * BLK), 0)
    col = lax.broadcasted_iota(jnp.int32, (BLK, 2 * BLK), 1)
    first_ok = col >= BLK

    def pieced(idx):
        per = BLK // BAND_PIECES
        return (idx & (per - 1)) * BAND_PIECES + (idx >> (per.bit_length() - 1))

    def tile(dist, max_dist, dist_scale, table_col):
        val = _bucket_values(dist, dist_scale, 0, max_dist, table_ref, table_col)
        val = jnp.where((dist >= 0) & (dist <= max_dist), val, NEG)
        return jnp.where(first_ok, val, NEG), val

    dist = row + BLK - col
    a_ref[0], a_ref[1] = tile(dist, A_WINDOW - 1, 1, h)
    for p, (w, d) in enumerate(B_PATTERNS):
        if p == PIECED_PATTERN:
            within = col & (BLK - 1)
            dist_p = pieced(row) + BLK - (pieced(within) + (col - within))
        else:
            dist_p = dist
        b_ref[p, 0], b_ref[p, 1] = tile(dist_p, w // d, d, n_a_heads + h)


def _diff_bias_kernel(table_ref, out_ref, *, tile, n_tiles, head_offset):
    h = pl.program_id(0)
    row = lax.broadcasted_iota(jnp.int32, (BLK, BLK), 0)
    col = lax.broadcasted_iota(jnp.int32, (BLK, BLK), 1)
    sub = tile // BLK
    pieces = {}
    for delta in range(-(sub - 1), n_tiles * sub):
        if delta < 0:
            pieces[delta] = jnp.full((BLK, BLK), NEG, F32)
            continue
        dist = row - col + delta * BLK
        val = _bucket_values(dist, 1, delta * BLK - (BLK - 1), delta * BLK + BLK - 1,
                             table_ref, head_offset + h)
        pieces[delta] = jnp.where(dist >= 0, val, NEG) if delta == 0 else val
    for dt in range(n_tiles):
        for a in range(sub):
            for b in range(sub):
                out_ref[dt, a * BLK:(a + 1) * BLK, b * BLK:(b + 1) * BLK] = pieces[dt * sub + a - b]


def _bias_tables(rel_table, n_a_heads, n_b_heads, n_c_heads, tile, n_tiles):
    smem = pl.BlockSpec(memory_space=pltpu.SMEM)
    bias_a, bias_b = pl.pallas_call(
        _band_bias_kernel,
        grid=(n_a_heads,),
        in_specs=[smem],
        out_specs=[pl.BlockSpec((2, BLK, 2 * BLK), lambda h: (0, h, 0)),
                   pl.BlockSpec((3, 2, BLK, 2 * BLK), lambda h: (0, 0, h, 0))],
        out_shape=[jax.ShapeDtypeStruct((2, n_a_heads * BLK, 2 * BLK), F32),
                   jax.ShapeDtypeStruct((3, 2, n_b_heads * BLK, 2 * BLK), F32)],
        name="band_bias",
    )(rel_table)
    bias_c = pl.pallas_call(
        functools.partial(_diff_bias_kernel, tile=tile, n_tiles=n_tiles,
                          head_offset=n_a_heads + n_b_heads),
        grid=(n_c_heads,),
        in_specs=[smem],
        out_specs=pl.BlockSpec((None, n_tiles, tile, tile), lambda h: (h, 0, 0, 0)),
        out_shape=jax.ShapeDtypeStruct((n_c_heads, n_tiles, tile, tile), F32),
        compiler_params=pltpu.CompilerParams(vmem_limit_bytes=VMEM_LIMIT),
        name="diff_bias",
    )(rel_table)
    return bias_a, bias_b, bias_c


def _split_bf16(v):
    hi = v.astype(BF16)
    lo = (v - hi.astype(F32)).astype(BF16)
    return hi, lo


def _mod_kernel(c_ref, w_ref, b_ref, out_ref):
    c = c_ref[...]
    s = c / (1.0 + jnp.exp(-c))
    s_hi, s_lo = _split_bf16(s)
    w_hi, w_lo = _split_bf16(w_ref[...])
    acc = jnp.dot(s_hi, w_hi, preferred_element_type=F32)
    acc += jnp.dot(s_lo, w_hi, preferred_element_type=F32)
    acc += jnp.dot(s_hi, w_lo, preferred_element_type=F32)
    out_ref[...] = acc + b_ref[...]


def _modulation(c, w_ada, b_ada):
    depth, d_model, n_out = w_ada.shape
    rows = 8
    c_pad = jnp.zeros((rows, d_model), F32).at[:c.shape[0]].set(c)
    tn = 768
    out = pl.pallas_call(
        _mod_kernel,
        grid=(depth, n_out // tn),
        in_specs=[pl.BlockSpec((rows, d_model), lambda l, j: (0, 0)),
                  pl.BlockSpec((None, d_model, tn), lambda l, j: (l, 0, j)),
                  pl.BlockSpec((None, 1, tn), lambda l, j: (l, 0, j))],
        out_specs=pl.BlockSpec((None, rows, tn), lambda l, j: (l, 0, j)),
        out_shape=jax.ShapeDtypeStruct((depth, rows, n_out), F32),
        name="adaln_mod",
    )(c_pad, w_ada, b_ada.reshape(depth, 1, n_out))
    return out[:, :c.shape[0]]


def _in_proj_kernel(x_ref, mod_ref, g_ref, w_ref, wkt_ref, perm_ref, *refs, widths, cm_groups):
    n_out = len(widths)
    col_refs = refs[:n_out]
    cm_refs = refs[n_out:n_out + len(cm_groups)]
    kt_ref = refs[n_out + len(cm_groups)]
    d_model = x_ref.shape[-1]
    x = x_ref[...]
    ms = jnp.mean(x * x, axis=-1, keepdims=True)
    shift = mod_ref[:, :d_model]
    scale = mod_ref[:, d_model:2 * d_model]
    h = (x * lax.rsqrt(ms + EPS)) * g_ref[...]
    h = (h * (1.0 + scale) + shift).astype(BF16)
    off = 0
    for gi, (ref, width) in enumerate(zip(col_refs, widths)):
        res = jnp.dot(h, w_ref[:, off:off + width],
                      preferred_element_type=F32).astype(ref.dtype)
        ref[...] = res
        if gi in cm_groups:
            cm_ref = cm_refs[cm_groups.index(gi)]
            n_cls, per_cls = cm_ref.shape[0], cm_ref.shape[1]
            perm = jnp.dot(perm_ref[...], res, preferred_element_type=F32).astype(cm_ref.dtype)
            for c in range(n_cls):
                cm_ref[c] = perm[c * per_cls:(c + 1) * per_cls]
        off += width
    kt_ref[...] = lax.dot_general(wkt_ref[...], h, (((1,), (1,)), ((), ())),
                                  preferred_element_type=F32).astype(kt_ref.dtype)


def _class_major_permutation(tm):
    per_cls = tm // N_CLASSES
    src = np.array([N_CLASSES * j + _class_residue(c)
                    for c in range(N_CLASSES) for j in range(per_cls)])
    return jnp.asarray(np.arange(tm)[None, :] == src[:, None], BF16)


def _in_proj(x, mod_l, g_pre_l, w_cols, w_kt, widths, cm_groups):
    bn, s, d_model = x.shape
    tm = TILE_M
    n_t = s // tm
    n_cols = w_cols.shape[1]
    kt_rows = w_kt.shape[0]
    per_cls = tm // N_CLASSES
    out_shape = [jax.ShapeDtypeStruct((bn, s, w), BF16) for w in widths]
    out_specs = [pl.BlockSpec((None, tm, w), lambda b, i: (b, i, 0)) for w in widths]
    for gi in cm_groups:
        out_shape.append(jax.ShapeDtypeStruct((bn, N_CLASSES, s // N_CLASSES, widths[gi]), BF16))
        out_specs.append(pl.BlockSpec((None, N_CLASSES, per_cls, widths[gi]),
                                      lambda b, i: (b, 0, i, 0)))
    out_shape.append(jax.ShapeDtypeStruct((bn, n_t, kt_rows, tm), BF16))
    out_specs.append(pl.BlockSpec((None, None, kt_rows, tm), lambda b, i: (b, i, 0, 0)))
    return pl.pallas_call(
        functools.partial(_in_proj_kernel, widths=widths, cm_groups=cm_groups),
        grid=(bn, n_t),
        in_specs=[pl.BlockSpec((None, tm, d_model), lambda b, i: (b, i, 0)),
                  pl.BlockSpec((None, 1, 3 * d_model), lambda b, i: (b, 0, 0)),
                  pl.BlockSpec((1, d_model), lambda b, i: (0, 0)),
                  pl.BlockSpec((d_model, n_cols), lambda b, i: (0, 0),
                               pipeline_mode=pl.Buffered(1)),
                  pl.BlockSpec((kt_rows, d_model), lambda b, i: (0, 0),
                               pipeline_mode=pl.Buffered(1)),
                  pl.BlockSpec((tm, tm), lambda b, i: (0, 0), pipeline_mode=pl.Buffered(1))],
        out_specs=out_specs,
        out_shape=out_shape,
        compiler_params=pltpu.CompilerParams(
            dimension_semantics=("parallel", "parallel"), vmem_limit_bytes=VMEM_LIMIT),
        name="in_proj",
    )(x, mod_l.reshape(bn, 1, 3 * d_model), g_pre_l.reshape(1, d_model), w_cols, w_kt,
      _class_major_permutation(tm))


def _lane_masks():
    lane = lax.broadcasted_iota(jnp.int32, (1, LANES), 1)
    even = lane < HEAD_DIM
    m_even = jnp.where(even, 1.0, 0.0).astype(BF16)
    m_odd = jnp.where(even, 0.0, 1.0).astype(BF16)
    return even, m_even, m_odd


def _banded_kernel(*refs, n_pairs, pairs_per_kv, has_sink, pieces):
    it = iter(refs)
    sink_ref = next(it) if has_sink else None
    q_ref, kp_ref, kc_ref, vp_ref, vc_ref, bias_ref, o_ref = (next(it) for _ in range(7))
    lse_ref = None if has_sink else next(it)
    k_buf, v_buf, s_buf = next(it), next(it), next(it)
    step = pl.program_id(2)
    if pieces == 1:
        nblk = q_ref.shape[0] // BLK
        k_buf[:BLK] = kp_ref[...]
        k_buf[BLK:] = kc_ref[...]
        v_buf[:BLK] = vp_ref[...]
        v_buf[BLK:] = vc_ref[...]
        q_src, o_dst, lse_dst = q_ref, o_ref, lse_ref
    else:
        q_src, o_dst, lse_dst = next(it), next(it), next(it)
        per = BLK // pieces
        nblk = q_ref.shape[1] // per
        for pc in range(pieces):
            k_buf[pc * per:(pc + 1) * per] = kp_ref[pc]
            v_buf[pc * per:(pc + 1) * per] = vp_ref[pc]
            for ib in range(nblk):
                dst = (ib * pieces + pc) * per
                src = slice(ib * per, (ib + 1) * per)
                q_src[dst:dst + per] = q_ref[pc, src]
                k_buf[BLK + dst:BLK + dst + per] = kc_ref[pc, src]
                v_buf[BLK + dst:BLK + dst + per] = vc_ref[pc, src]
    even, m_even, m_odd = _lane_masks()
    lane = lax.broadcasted_iota(jnp.int32, (BLK, LANES), 1)
    top_rows = lax.broadcasted_iota(jnp.int32, (2 * BLK, 1), 0) < BLK
    ones = jnp.ones((2 * BLK, LANES), BF16)

    for ib in range(nblk):
        r0 = ib * BLK
        variant = jnp.where(step == 0, 0, 1) if ib == 0 else 1
        lse_tile = jnp.ones((BLK, LANES), F32)
        for j in range(n_pairs):
            g = j // pairs_per_kv
            q2 = q_src[pl.ds(r0, BLK), j * LANES:(j + 1) * LANES]
            qs = jnp.concatenate([q2 * m_even, q2 * m_odd], axis=0)
            k2 = k_buf[pl.ds(r0, 2 * BLK), g * LANES:(g + 1) * LANES]
            s = lax.dot_general(qs, k2, (((1,), (1,)), ((), ())), preferred_element_type=F32)
            s_buf[ib * n_pairs + j] = s + bias_ref[variant, j * 2 * BLK:(j + 1) * 2 * BLK, :]
        for j in range(n_pairs):
            g = j // pairs_per_kv
            v2 = v_buf[pl.ds(r0, 2 * BLK), g * LANES:(g + 1) * LANES]
            s = s_buf[ib * n_pairs + j]
            m = jnp.max(s, axis=-1, keepdims=True)
            if has_sink:
                sk = jnp.where(top_rows, sink_ref[2 * j], sink_ref[2 * j + 1]) * LOG2E
                m = jnp.maximum(m, sk)
            e = jnp.exp2(s - m)
            ox = jnp.dot(e.astype(BF16), jnp.concatenate([v2, ones], axis=1),
                         preferred_element_type=F32)
            ov, den = ox[:, :LANES], ox[:, LANES:]
            if has_sink:
                ov = ov / (den + jnp.exp2(sk - m))
            o_dst[pl.ds(r0, BLK), j * LANES:(j + 1) * LANES] = (
                jnp.where(even, ov[:BLK], ov[BLK:]).astype(o_dst.dtype))
            if lse_dst is not None:
                for hh, half in ((2 * j, slice(0, BLK)), (2 * j + 1, slice(BLK, 2 * BLK))):
                    lse_tile = jnp.where(lane == hh, m[half], lse_tile)
                    lse_tile = jnp.where(lane == HEAD_DIM + hh, den[half], lse_tile)
        if lse_dst is not None:
            lse_dst[pl.ds(r0, BLK), :] = lse_tile

    if pieces > 1:
        for pc in range(pieces):
            for ib in range(nblk):
                dst = (ib * pieces + pc) * per
                o_ref[pc, ib * per:(ib + 1) * per] = o_dst[dst:dst + per]
                lse_ref[pc, ib * per:(ib + 1) * per] = lse_dst[dst:dst + per]


def _banded_attention(q, k, v, bias, pairs_per_kv, layout, sinks=None):
    wq, wkv = q.shape[-1], k.shape[-1]
    bn = q.shape[0]
    n_pairs = wq // LANES
    has_sink = sinks is not None
    if layout == "natural":
        seq = q.shape[1]
        nblk = min(BAND_BLOCKS, seq // BLK)
        rows = nblk * BLK
        grid = (bn, 1, seq // rows)
        cur = lambda b, r, n: (b, n, 0)
        prev = lambda b, r, n: (b, jnp.maximum(n * nblk - 1, 0), 0)
        cur_blk = lambda w: (None, rows, w)
        prev_blk = lambda w: (None, BLK, w)
        pieces = 1
    elif layout == "classes":
        n_cls, seq = q.shape[1], q.shape[2]
        nblk = min(BAND_BLOCKS, seq // BLK)
        rows = nblk * BLK
        grid = (bn, n_cls, seq // rows)
        cur = lambda b, r, n: (b, r, n, 0)
        prev = lambda b, r, n: (b, r, jnp.maximum(n * nblk - 1, 0), 0)
        cur_blk = lambda w: (None, None, rows, w)
        prev_blk = lambda w: (None, None, BLK, w)
        pieces = 1
    else:
        n_cls, seq = q.shape[1], q.shape[2]
        pieces = BAND_PIECES
        per = BLK // pieces
        nblk = pieces
        rows = BLK
        grid = (bn, n_cls // pieces, seq // rows)
        cur = lambda b, r, n: (b, r, n, 0)
        prev = lambda b, r, n: (b, r, jnp.maximum(n * (rows // per) - 1, 0), 0)
        cur_blk = lambda w: (None, pieces, rows, w)
        prev_blk = lambda w: (None, pieces, per, w)
    tot = nblk * BLK
    in_specs = [pl.BlockSpec(cur_blk(wq), cur),
                pl.BlockSpec(prev_blk(wkv), prev),
                pl.BlockSpec(cur_blk(wkv), cur),
                pl.BlockSpec(prev_blk(wkv), prev),
                pl.BlockSpec(cur_blk(wkv), cur),
                pl.BlockSpec(bias.shape, lambda b, r, n: (0, 0, 0))]
    args = [q, k, k, v, v, bias]
    out_specs = [pl.BlockSpec(cur_blk(wq), cur)]
    out_shape = [jax.ShapeDtypeStruct(q.shape, BF16)]
    scratch = [pltpu.VMEM((tot + BLK, wkv), BF16), pltpu.VMEM((tot + BLK, wkv), BF16),
               pltpu.VMEM((nblk * n_pairs, 2 * BLK, 2 * BLK), F32)]
    if has_sink:
        in_specs.insert(0, pl.BlockSpec(memory_space=pltpu.SMEM))
        args.insert(0, sinks)
    else:
        out_specs.append(pl.BlockSpec(cur_blk(LANES), cur))
        out_shape.append(jax.ShapeDtypeStruct(q.shape[:-1] + (LANES,), F32))
    if pieces > 1:
        scratch += [pltpu.VMEM((tot, wq), BF16), pltpu.VMEM((tot, wq), BF16),
                    pltpu.VMEM((tot, LANES), F32)]
    outs = pl.pallas_call(
        functools.partial(_banded_kernel, n_pairs=n_pairs, pairs_per_kv=pairs_per_kv,
                          has_sink=has_sink, pieces=pieces),
        grid=grid,
        in_specs=in_specs,
        out_specs=out_specs,
        out_shape=out_shape,
        scratch_shapes=scratch,
        compiler_params=pltpu.CompilerParams(
            dimension_semantics=("parallel", "parallel", "arbitrary"),
            vmem_limit_bytes=VMEM_LIMIT),
        name="banded_sink" if has_sink else f"banded_{layout}",
    )(*args)
    return outs[0] if has_sink else tuple(outs)


def _diff_kernel(far_ref, q_ref, kt_ref, v_ref, bias_ref, lam_ref, g_ref, o_ref,
                 qs_s, vx_s, m_s, acc_s, s_s, *, lam_init, n_near):
    h = pl.program_id(1)
    i = pl.program_id(2)
    t = q_ref.shape[0]
    n_rep = t // LANES

    @pl.when(i == 0)
    def _():
        vx_s[:, :LANES] = v_ref[...]
        vx_s[:, LANES:] = jnp.ones((vx_s.shape[0], LANES), BF16)

    _, m_even, m_odd = _lane_masks()
    q2 = q_ref[...]
    qs_s[:t] = q2 * m_even
    qs_s[t:] = q2 * m_odd
    m_s[...] = jnp.full(m_s.shape, NEG, F32)
    acc_s[...] = jnp.zeros(acc_s.shape, F32)
    far_bias = far_ref[h] * LOG2E

    def sweep(j0, n_tiles, near):
        k0 = pl.multiple_of(j0 * t, t)
        kts = [kt_ref[j0 + a] for a in range(n_tiles)]
        vx = vx_s[pl.ds(k0, n_tiles * t), :]
        width = n_tiles * t
        for r0 in range(0, 2 * t, DIFF_ROWS):
            rows = pl.ds(r0, DIFF_ROWS)
            qc = qs_s[rows, :]
            for a, kt in enumerate(kts):
                part = jnp.dot(qc, kt, preferred_element_type=F32)
                if near:
                    part = part + bias_ref[jnp.minimum(i - j0 - a, n_near),
                                           pl.ds(r0 % t, DIFF_ROWS), :]
                s_s[rows, a * t:(a + 1) * t] = part
        shift = 0.0 if near else far_bias
        for r0 in range(0, 2 * t, DIFF_ROWS):
            rows = pl.ds(r0, DIFF_ROWS)
            s = s_s[rows, :width]
            m_cur = jnp.max(s, axis=-1, keepdims=True) + shift
            m_old = m_s[rows, :]
            m_new = jnp.maximum(m_old, m_cur)
            alpha = jnp.exp2(m_old - m_new)
            m_sub = m_new - shift
            p = jnp.exp2(s - jnp.concatenate([m_sub] * (n_tiles * n_rep), axis=1))
            pv = jnp.dot(p.astype(BF16), vx, preferred_element_type=F32)
            acc_s[rows, :] = jnp.concatenate([alpha, alpha], axis=1) * acc_s[rows, :] + pv
            m_s[rows, :] = m_new

    n_far = jnp.maximum(i - (n_near - 1), 0)
    n_quads = n_far // DIFF_MAX_TILES
    j_pairs = DIFF_MAX_TILES * n_quads
    n_far_pairs = (n_far - j_pairs) >> 1
    j_near0 = j_pairs + 2 * n_far_pairs
    n_near_quads = jnp.where(i + 1 - j_near0 >= DIFF_MAX_TILES, 1, 0)
    j_near = j_near0 + DIFF_MAX_TILES * n_near_quads
    odd = (i + 1 - j_near) & 1

    def far_quad(jj, carry):
        sweep(DIFF_MAX_TILES * jj, DIFF_MAX_TILES, False)
        return carry

    def far_pair(jj, carry):
        sweep(j_pairs + 2 * jj, 2, False)
        return carry

    def near_quad(jj, carry):
        sweep(j_near0, DIFF_MAX_TILES, True)
        return carry

    lax.fori_loop(0, n_quads, far_quad, 0)
    lax.fori_loop(0, n_far_pairs, far_pair, 0)
    lax.fori_loop(0, n_near_quads, near_quad, 0)

    @pl.when(odd == 1)
    def _():
        sweep(j_near, 1, True)

    def near_pair(kk, carry):
        sweep(j_near + odd + 2 * kk, 2, True)
        return carry

    lax.fori_loop(0, (i + 1 - j_near) >> 1, near_pair, 0)

    acc = acc_s[...]
    o = acc[:, :LANES] / acc[:, LANES:]
    lam = (jnp.exp(jnp.sum(lam_ref[0:1, :] * lam_ref[1:2, :], axis=-1, keepdims=True))
           - jnp.exp(jnp.sum(lam_ref[2:3, :] * lam_ref[3:4, :], axis=-1, keepdims=True)) + lam_init)
    y = o[:t] - lam * o[t:]
    y = y * lax.rsqrt(jnp.mean(y * y, axis=-1, keepdims=True) + EPS)
    o_ref[...] = (y * g_ref[...] * (1.0 - lam_init)).astype(o_ref.dtype)


def _diff_attention(q, kt, v, bias_c, far_bias, lam_params, g_sub_l, lam_init):
    bn, s, w = q.shape
    t = TILE_M
    n_t = s // t
    n_heads = w // LANES
    n_near = bias_c.shape[1] - 1
    return pl.pallas_call(
        functools.partial(_diff_kernel, lam_init=lam_init, n_near=n_near),
        grid=(bn, n_heads, n_t),
        in_specs=[pl.BlockSpec(memory_space=pltpu.SMEM),
                  pl.BlockSpec((None, t, LANES), lambda b, h, i: (b, i, h)),
                  pl.BlockSpec((None, n_t, LANES, t), lambda b, h, i: (b, 0, h, 0)),
                  pl.BlockSpec((None, s, LANES), lambda b, h, i: (b, 0, h)),
                  pl.BlockSpec((None, n_near + 1, t, t), lambda b, h, i: (h, 0, 0, 0)),
                  pl.BlockSpec((4, HEAD_DIM), lambda b, h, i: (0, 0)),
                  pl.BlockSpec((1, LANES), lambda b, h, i: (0, 0))],
        out_specs=pl.BlockSpec((None, t, LANES), lambda b, h, i: (b, i, h)),
        out_shape=jax.ShapeDtypeStruct((bn, s, w), BF16),
        scratch_shapes=[pltpu.VMEM((2 * t, LANES), BF16),
                        pltpu.VMEM((s, 2 * LANES), BF16),
                        pltpu.VMEM((2 * t, LANES), F32),
                        pltpu.VMEM((2 * t, 2 * LANES), F32),
                        pltpu.VMEM((2 * t, DIFF_MAX_TILES * t), F32)],
        compiler_params=pltpu.CompilerParams(
            dimension_semantics=("parallel", "parallel", "arbitrary"),
            vmem_limit_bytes=VMEM_LIMIT),
        name="diff_attn",
    )(far_bias, q, kt, v, bias_c, lam_params, g_sub_l.reshape(1, LANES))


def _out_proj_kernel(x_ref, mod_ref, g_ref, w_ref, e_ref, unperm_ref, ya_ref, o1_ref, o4_ref,
                     o16_ref, l1_ref, l4_ref, l16_ref, yc_ref, z_ref, out_ref, l4_scr, l16_scr):
    d_model = x_ref.shape[-1]
    wa = ya_ref.shape[-1]
    wb = o1_ref.shape[-1]
    z = z_ref[...].astype(F32)
    sz = z / (1.0 + jnp.exp(-z))

    def to_natural(cm_ref, scr):
        n_cls, per_cls = cm_ref.shape[0], cm_ref.shape[1]
        for k in range(scr.shape[0]):
            for c in range(n_cls):
                scr[k, pl.ds(_class_residue(c), per_cls, stride=n_cls), :] = cm_ref[
                    c, :, k * LANES:(k + 1) * LANES].astype(F32)
        return jnp.concatenate([scr[k] for k in range(scr.shape[0])], axis=1)

    def rows_to_natural(cm_ref):
        n_cls, per_cls, w = cm_ref.shape
        return jnp.dot(unperm_ref[...], cm_ref[...].reshape(n_cls * per_cls, w),
                       preferred_element_type=F32)

    o_parts = [o1_ref[...].astype(F32), rows_to_natural(o4_ref), rows_to_natural(o16_ref)]
    stats = [l1_ref[...], to_natural(l4_ref, l4_scr), to_natural(l16_ref, l16_scr)]
    head_lane = lax.broadcasted_iota(jnp.int32, stats[0].shape, 1) < wb // HEAD_DIM
    maxes = [jnp.where(head_lane, st, 0.0) for st in stats]
    dens = [jnp.where(head_lane, pltpu.roll(st, HEAD_DIM, 1), 1.0) for st in stats]
    lses = [mxp + jnp.log2(dn) for mxp, dn in zip(maxes, dens)]
    mx = jnp.maximum(jnp.maximum(lses[0], lses[1]), lses[2])
    ws = [jnp.exp2(l - mx) for l in lses]
    tot = ws[0] + ws[1] + ws[2]
    yb = jnp.zeros((x_ref.shape[0], wb), F32)
    for wgt, dn, o_part in zip(ws, dens, o_parts):
        spread = jnp.dot(jnp.concatenate(_split_bf16(wgt / (tot * dn)), axis=1), e_ref[...],
                         preferred_element_type=F32)
        yb = yb + spread * o_part

    ga = (ya_ref[...].astype(F32) * sz[:, :wa]).astype(BF16)
    gb = (yb * sz[:, wa:wa + wb]).astype(BF16)
    gc = (yc_ref[...].astype(F32) * sz[:, wa + wb:]).astype(BF16)
    y = jnp.dot(ga, w_ref[:wa], preferred_element_type=F32)
    y += jnp.dot(gb, w_ref[wa:wa + wb], preferred_element_type=F32)
    y += jnp.dot(gc, w_ref[wa + wb:], preferred_element_type=F32)
    r = (y * lax.rsqrt(jnp.mean(y * y, axis=-1, keepdims=True) + EPS)) * g_ref[...]
    gate = mod_ref[:, 2 * d_model:]
    out_ref[...] = x_ref[...] + gate * r


def _out_proj(x, mod_l, g_post_l, w_out_l, spread, ya, ob, lb, yc, z):
    bn, s, d_model = x.shape
    tm = TILE_M
    per_cls = tm // N_CLASSES
    wb = ob[0].shape[-1]
    row = lambda w: pl.BlockSpec((None, tm, w), lambda b, i: (b, i, 0))
    cm = lambda w: pl.BlockSpec((None, N_CLASSES, per_cls, w), lambda b, i: (b, 0, i, 0))
    whole = lambda a: pl.BlockSpec(a.shape, lambda b, i: (0,) * a.ndim,
                                   pipeline_mode=pl.Buffered(1))
    g2 = g_post_l.reshape(1, d_model)
    unperm = _class_major_permutation(tm).T
    return pl.pallas_call(
        _out_proj_kernel,
        grid=(bn, s // tm),
        in_specs=[row(d_model),
                  pl.BlockSpec((None, 1, 3 * d_model), lambda b, i: (b, 0, 0)),
                  whole(g2), whole(w_out_l), whole(spread), whole(unperm),
                  row(ya.shape[-1]),
                  row(wb), cm(wb), cm(wb),
                  row(LANES), cm(LANES), cm(LANES),
                  row(yc.shape[-1]), row(z.shape[-1])],
        out_specs=row(d_model),
        out_shape=jax.ShapeDtypeStruct(x.shape, x.dtype),
        scratch_shapes=[pltpu.VMEM((1, tm, LANES), F32), pltpu.VMEM((1, tm, LANES), F32)],
        compiler_params=pltpu.CompilerParams(
            dimension_semantics=("parallel", "parallel"), vmem_limit_bytes=VMEM_LIMIT),
        name="out_proj",
    )(x, mod_l.reshape(bn, 1, 3 * d_model), g2, w_out_l, spread, unperm, ya, *ob, *lb, yc, z)


def _prepare_w_in(w, sizes, n_kv_a):
    offs = np.concatenate([[0], np.cumsum(sizes)])
    grp = lambda i: w[:, int(offs[i]):int(offs[i + 1])]
    aq, ak, av, bq, bk, bv, cq, ck, cv, z = [grp(i) for i in range(10)]

    def dup(m):
        parts = []
        for g in range(n_kv_a):
            head = m[:, g * HEAD_DIM:(g + 1) * HEAD_DIM]
            parts += [head, head]
        return jnp.concatenate(parts, axis=1)

    groups = [aq * Q_SCALE, dup(ak), dup(av), bq * Q_SCALE, bk, bv, cq * Q_SCALE, cv, z]
    widths = tuple(int(g.shape[1]) for g in groups)
    w_cols = jnp.concatenate(groups, axis=1).astype(BF16)
    return w_cols, ck.T.astype(BF16), widths


def kernel(x, c, rel_table, w_in, w_out, w_ada, b_ada, g_pre, g_post, a_sinks,
           lam_q1, lam_k1, lam_q2, lam_k2, g_sub):
    bn, s, d_model = x.shape
    depth = w_in.shape[0]
    d_mix = w_out.shape[1]
    a_width = b_width = 3 * d_mix // 8
    c_width = d_mix // 4
    n_a_heads = a_width // HEAD_DIM
    n_kv_a = n_a_heads // 4
    n_b_heads = b_width // HEAD_DIM
    n_c_heads = c_width // (2 * HEAD_DIM)
    sizes = (a_width, n_kv_a * HEAD_DIM, n_kv_a * HEAD_DIM, b_width, b_width, b_width,
             c_width, c_width, c_width, d_mix)
    assert w_in.shape[2] == sum(sizes) and rel_table.shape == (REL_BUCKETS,
                                                                n_a_heads + n_b_heads + n_c_heads)
    assert n_a_heads == n_b_heads and s % (N_CLASSES * BLK) == 0
    assert [d for _, d in B_PATTERNS] == [1, BAND_PIECES, N_CLASSES]
    assert s % TILE_M == 0 and TILE_M % BLK == 0

    n_near = -(-(FAR_DIST - 1) // TILE_M) + 1
    bias_a, bias_b, bias_c = _bias_tables(rel_table, n_a_heads, n_b_heads, n_c_heads,
                                          TILE_M, n_near + 1)
    far_bias = rel_table[REL_BUCKETS - 1, n_a_heads + n_b_heads:]
    mod = _modulation(c, w_ada, b_ada)

    head_of_lane = np.arange(b_width) // HEAD_DIM
    spread = jnp.asarray(np.arange(2 * LANES)[:, None] % LANES == head_of_lane[None, :], BF16)

    for l in range(depth):
        w_cols, w_kt, widths = _prepare_w_in(w_in[l], sizes, n_kv_a)
        (aq, ak, av, bq, bk, bv, cq, cv, z, bq_cm, bk_cm, bv_cm, ckt) = _in_proj(
            x, mod[l], g_pre[l], w_cols, w_kt, widths, cm_groups=(3, 4, 5))

        ya = _banded_attention(aq, ak, av, bias_a, 2, "natural", sinks=a_sinks[l])
        o1, l1 = _banded_attention(bq, bk, bv, bias_b[0], 1, "natural")
        o4, l4 = _banded_attention(bq_cm, bk_cm, bv_cm, bias_b[1], 1, "pieces")
        o16, l16 = _banded_attention(bq_cm, bk_cm, bv_cm, bias_b[2], 1, "classes")
        ob, lb = (o1, o4, o16), (l1, l4, l16)

        lam_init = 0.8 - 0.6 * math.exp(-0.3 * l)
        lam_params = jnp.stack([lam_q1[l], lam_k1[l], lam_q2[l], lam_k2[l]]).astype(F32)
        yc = _diff_attention(cq, ckt, cv, bias_c, far_bias, lam_params, g_sub[l], lam_init)

        x = _out_proj(x, mod[l], g_post[l], w_out[l].astype(BF16), spread, ya, ob, lb, yc, z)
    return x
```

```python
import functools
import math

import numpy as np
import jax
import jax.numpy as jnp
from jax import lax
from jax.experimental import pallas as pl
from jax.experimental.pallas import tpu as pltpu

F32 = jnp.float32
BF16 = jnp.bfloat16

HEAD_DIM = 64
LANES = 128
BLK = 128
A_WINDOW = 128
B_PATTERNS = ((128, 1), (512, 4), (2048, 16))
REL_BUCKETS = 32
REL_MAX_DIST = 2048
EPS = 1e-6
NEG = -1e30
LOG2E = 1.4426950408889634
Q_SCALE = LOG2E / math.sqrt(HEAD_DIM)

TILE_M = 512
BAND_BLOCKS = 8
N_CLASSES = 16
BAND_PIECES = 4
PIECED_PATTERN = 1
DIFF_ROWS = 128
DIFF_MAX_TILES = 4
VMEM_LIMIT = 56 * 1024 * 1024


def _class_residue(c):
    return c // BAND_PIECES + (N_CLASSES // BAND_PIECES) * (c % BAND_PIECES)


def _rel_bucket_np(n):
    n = np.maximum(np.asarray(n, np.int64), 0)
    max_exact = REL_BUCKETS // 2
    nf = np.maximum(n, 1).astype(np.float32)
    large = max_exact + (np.log(nf / np.float32(max_exact))
                         / np.float32(math.log(REL_MAX_DIST / max_exact))
                         * np.float32(REL_BUCKETS - max_exact)).astype(np.int32)
    large = np.minimum(large, REL_BUCKETS - 1)
    return np.where(n < max_exact, n, large)


_BUCKET_OF = _rel_bucket_np(np.arange(0, 1 << 18))
_THR = [0] + [int(np.argmax(_BUCKET_OF >= b)) for b in range(1, REL_BUCKETS)]
FAR_DIST = _THR[REL_BUCKETS - 1]


def _bucket_values(dist, dist_scale, lo, hi, table_ref, col):
    b_lo = int(_BUCKET_OF[max(lo, 0) * dist_scale])
    b_hi = int(_BUCKET_OF[max(hi, 0) * dist_scale])
    val = jnp.full(dist.shape, table_ref[b_lo, col] * LOG2E, F32)
    for b in range(b_lo + 1, b_hi + 1):
        thr = -(-_THR[b] // dist_scale)
        val = jnp.where(dist >= thr, table_ref[b, col] * LOG2E, val)
    return val


def _band_bias_kernel(table_ref, a_ref, b_ref):
    h = pl.program_id(0)
    n_a_heads = pl.num_programs(0)
    row = lax.broadcasted_iota(jnp.int32, (BLK, 2 * BLK), 0)
    col = lax.broadcasted_iota(jnp.int32, (BLK, 2 * BLK), 1)
    first_ok = col >= BLK

    def pieced(idx):
        per = BLK // BAND_PIECES
        return (idx & (per - 1)) * BAND_PIECES + (idx >> (per.bit_length() - 1))

    def tile(dist, max_dist, dist_scale, table_col):
        val = _bucket_values(dist, dist_scale, 0, max_dist, table_ref, table_col)
        val = jnp.where((dist >= 0) & (dist <= max_dist), val, NEG)
        return jnp.where(first_ok, val, NEG), val

    dist = row + BLK - col
    a_ref[0], a_ref[1] = tile(dist, A_WINDOW - 1, 1, h)
    for p, (w, d) in enumerate(B_PATTERNS):
        if p == PIECED_PATTERN:
            within = col & (BLK - 1)
            dist_p = pieced(row) + BLK - (pieced(within) + (col - within))
        else:
            dist_p = dist
        b_ref[p, 0], b_ref[p, 1] = tile(dist_p, w // d, d, n_a_heads + h)


def _diff_bias_kernel(table_ref, out_ref, *, tile, n_tiles, head_offset):
    h = pl.program_id(0)
    row = lax.broadcasted_iota(jnp.int32, (BLK, BLK), 0)
    col = lax.broadcasted_iota(jnp.int32, (BLK, BLK), 1)
    sub = tile // BLK
    pieces = {}
    for delta in range(-(sub - 1), n_tiles * sub):
        if delta < 0:
            pieces[delta] = jnp.full((BLK, BLK), NEG, F32)
            continue
        dist = row - col + delta * BLK
        val = _bucket_values(dist, 1, delta * BLK - (BLK - 1), delta * BLK + BLK - 1,
                             table_ref, head_offset + h)
        pieces[delta] = jnp.where(dist >= 0, val, NEG) if delta == 0 else val
    for dt in range(n_tiles):
        for a in range(sub):
            for b in range(sub):
                out_ref[dt, a * BLK:(a + 1) * BLK, b * BLK:(b + 1) * BLK] = pieces[dt * sub + a - b]


def _bias_tables(rel_table, n_a_heads, n_b_heads, n_c_heads, tile, n_tiles):
    smem = pl.BlockSpec(memory_space=pltpu.SMEM)
    bias_a, bias_b = pl.pallas_call(
        _band_bias_kernel,
        grid=(n_a_heads,),
        in_specs=[smem],
        out_specs=[pl.BlockSpec((2, BLK, 2 * BLK), lambda h: (0, h, 0)),
                   pl.BlockSpec((3, 2, BLK, 2 * BLK), lambda h: (0, 0, h, 0))],
        out_shape=[jax.ShapeDtypeStruct((2, n_a_heads * BLK, 2 * BLK), F32),
                   jax.ShapeDtypeStruct((3, 2, n_b_heads * BLK, 2 * BLK), F32)],
        name="band_bias",
    )(rel_table)
    bias_c = pl.pallas_call(
        functools.partial(_diff_bias_kernel, tile=tile, n_tiles=n_tiles,
                          head_offset=n_a_heads + n_b_heads),
        grid=(n_c_heads,),
        in_specs=[smem],
        out_specs=pl.BlockSpec((None, n_tiles, tile, tile), lambda h: (h, 0, 0, 0)),
        out_shape=jax.ShapeDtypeStruct((n_c_heads, n_tiles, tile, tile), F32),
        compiler_params=pltpu.CompilerParams(vmem_limit_bytes=VMEM_LIMIT),
        name="diff_bias",
    )(rel_table)
    return bias_a, bias_b, bias_c


def _split_bf16(v):
    hi = v.astype(BF16)
    lo = (v - hi.astype(F32)).astype(BF16)
    return hi, lo


def _mod_kernel(c_ref, w_ref, b_ref, out_ref):
    c = c_ref[...]
    s = c / (1.0 + jnp.exp(-c))
    s_hi, s_lo = _split_bf16(s)
    w_hi, w_lo = _split_bf16(w_ref[...])
    acc = jnp.dot(s_hi, w_hi, preferred_element_type=F32)
    acc += jnp.dot(s_lo, w_hi, preferred_element_type=F32)
    acc += jnp.dot(s_hi, w_lo, preferred_element_type=F32)
    out_ref[...] = acc + b_ref[...]


def _modulation(c, w_ada, b_ada):
    depth, d_model, n_out = w_ada.shape
    rows = 8
    c_pad = jnp.zeros((rows, d_model), F32).at[:c.shape[0]].set(c)
    tn = 768
    out = pl.pallas_call(
        _mod_kernel,
        grid=(depth, n_out // tn),
        in_specs=[pl.BlockSpec((rows, d_model), lambda l, j: (0, 0)),
                  pl.BlockSpec((None, d_model, tn), lambda l, j: (l, 0, j)),
                  pl.BlockSpec((None, 1, tn), lambda l, j: (l, 0, j))],
        out_specs=pl.BlockSpec((None, rows, tn), lambda l, j: (l, 0, j)),
        out_shape=jax.ShapeDtypeStruct((depth, rows, n_out), F32),
        name="adaln_mod",
    )(c_pad, w_ada, b_ada.reshape(depth, 1, n_out))
    return out[:, :c.shape[0]]


def _in_proj_kernel(x_ref, mod_ref, g_ref, w_ref, wkt_ref, perm_ref, *refs, widths, cm_groups):
    n_out = len(widths)
    col_refs = refs[:n_out]
    cm_refs = refs[n_out:n_out + len(cm_groups)]
    kt_ref = refs[n_out + len(cm_groups)]
    d_model = x_ref.shape[-1]
    x = x_ref[...]
    ms = jnp.mean(x * x, axis=-1, keepdims=True)
    shift = mod_ref[:, :d_model]
    scale = mod_ref[:, d_model:2 * d_model]
    h = (x * lax.rsqrt(ms + EPS)) * g_ref[...]
    h = (h * (1.0 + scale) + shift).astype(BF16)
    off = 0
    for gi, (ref, width) in enumerate(zip(col_refs, widths)):
        res = jnp.dot(h, w_ref[:, off:off + width],
                      preferred_element_type=F32).astype(ref.dtype)
        ref[...] = res
        if gi in cm_groups:
            cm_ref = cm_refs[cm_groups.index(gi)]
            n_cls, per_cls = cm_ref.shape[0], cm_ref.shape[1]
            perm = jnp.dot(perm_ref[...], res, preferred_element_type=F32).astype(cm_ref.dtype)
            for c in range(n_cls):
                cm_ref[c] = perm[c * per_cls:(c + 1) * per_cls]
        off += width
    kt_ref[...] = lax.dot_general(wkt_ref[...], h, (((1,), (1,)), ((), ())),
                                  preferred_element_type=F32).astype(kt_ref.dtype)


def _class_major_permutation(tm):
    per_cls = tm // N_CLASSES
    src = np.array([N_CLASSES * j + _class_residue(c)
                    for c in range(N_CLASSES) for j in range(per_cls)])
    return jnp.asarray(np.arange(tm)[None, :] == src[:, None], BF16)


def _in_proj(x, mod_l, g_pre_l, w_cols, w_kt, widths, cm_groups):
    bn, s, d_model = x.shape
    tm = TILE_M
    n_t = s // tm
    n_cols = w_cols.shape[1]
    kt_rows = w_kt.shape[0]
    per_cls = tm // N_CLASSES
    out_shape = [jax.ShapeDtypeStruct((bn, s, w), BF16) for w in widths]
    out_specs = [pl.BlockSpec((None, tm, w), lambda b, i: (b, i, 0)) for w in widths]
    for gi in cm_groups:
        out_shape.append(jax.ShapeDtypeStruct((bn, N_CLASSES, s // N_CLASSES, widths[gi]), BF16))
        out_specs.append(pl.BlockSpec((None, N_CLASSES, per_cls, widths[gi]),
                                      lambda b, i: (b, 0, i, 0)))
    out_shape.append(jax.ShapeDtypeStruct((bn, n_t, kt_rows, tm), BF16))
    out_specs.append(pl.BlockSpec((None, None, kt_rows, tm), lambda b, i: (b, i, 0, 0)))
    return pl.pallas_call(
        functools.partial(_in_proj_kernel, widths=widths, cm_groups=cm_groups),
        grid=(bn, n_t),
        in_specs=[pl.BlockSpec((None, tm, d_model), lambda b, i: (b, i, 0)),
                  pl.BlockSpec((None, 1, 3 * d_model), lambda b, i: (b, 0, 0)),
                  pl.BlockSpec((1, d_model), lambda b, i: (0, 0)),
                  pl.BlockSpec((d_model, n_cols), lambda b, i: (0, 0),
                               pipeline_mode=pl.Buffered(1)),
                  pl.BlockSpec((kt_rows, d_model), lambda b, i: (0, 0),
                               pipeline_mode=pl.Buffered(1)),
                  pl.BlockSpec((tm, tm), lambda b, i: (0, 0), pipeline_mode=pl.Buffered(1))],
        out_specs=out_specs,
        out_shape=out_shape,
        compiler_params=pltpu.CompilerParams(
            dimension_semantics=("parallel", "parallel"), vmem_limit_bytes=VMEM_LIMIT),
        name="in_proj",
    )(x, mod_l.reshape(bn, 1, 3 * d_model), g_pre_l.reshape(1, d_model), w_cols, w_kt,
      _class_major_permutation(tm))


def _lane_masks():
    lane = lax.broadcasted_iota(jnp.int32, (1, LANES), 1)
    even = lane < HEAD_DIM
    m_even = jnp.where(even, 1.0, 0.0).astype(BF16)
    m_odd = jnp.where(even, 0.0, 1.0).astype(BF16)
    return even, m_even, m_odd


def _banded_kernel(*refs, n_pairs, pairs_per_kv, has_sink, pieces):
    it = iter(refs)
    sink_ref = next(it) if has_sink else None
    q_ref, kp_ref, kc_ref, vp_ref, vc_ref, bias_ref, o_ref = (next(it) for _ in range(7))
    lse_ref = None if has_sink else next(it)
    k_buf, v_buf, s_buf = next(it), next(it), next(it)
    step = pl.program_id(2)
    if pieces == 1:
        nblk = q_ref.shape[0] // BLK
        k_buf[:BLK] = kp_ref[...]
        k_buf[BLK:] = kc_ref[...]
        v_buf[:BLK] = vp_ref[...]
        v_buf[BLK:] = vc_ref[...]
        q_src, o_dst, lse_dst = q_ref, o_ref, lse_ref
    else:
        q_src, o_dst, lse_dst = next(it), next(it), next(it)
        per = BLK // pieces
        nblk = q_ref.shape[1] // per
        for pc in range(pieces):
            k_buf[pc * per:(pc + 1) * per] = kp_ref[pc]
            v_buf[pc * per:(pc + 1) * per] = vp_ref[pc]
            for ib in range(nblk):
                dst = (ib * pieces + pc) * per
                src = slice(ib * per, (ib + 1) * per)
                q_src[dst:dst + per] = q_ref[pc, src]
                k_buf[BLK + dst:BLK + dst + per] = kc_ref[pc, src]
                v_buf[BLK + dst:BLK + dst + per] = vc_ref[pc, src]
    even, m_even, m_odd = _lane_masks()
    lane = lax.broadcasted_iota(jnp.int32, (BLK, LANES), 1)
    top_rows = lax.broadcasted_iota(jnp.int32, (2 * BLK, 1), 0) < BLK
    ones = jnp.ones((2 * BLK, LANES), BF16)

    for ib in range(nblk):
        r0 = ib * BLK
        variant = jnp.where(step == 0, 0, 1) if ib == 0 else 1
        lse_tile = jnp.ones((BLK, LANES), F32)
        for j in range(n_pairs):
            g = j // pairs_per_kv
            q2 = q_src[pl.ds(r0, BLK), j * LANES:(j + 1) * LANES]
            qs = jnp.concatenate([q2 * m_even, q2 * m_odd], axis=0)
            k2 = k_buf[pl.ds(r0, 2 * BLK), g * LANES:(g + 1) * LANES]
            s = lax.dot_general(qs, k2, (((1,), (1,)), ((), ())), preferred_element_type=F32)
            s_buf[ib * n_pairs + j] = s + bias_ref[variant, j * 2 * BLK:(j + 1) * 2 * BLK, :]
        for j in range(n_pairs):
            g = j // pairs_per_kv
            v2 = v_buf[pl.ds(r0, 2 * BLK), g * LANES:(g + 1) * LANES]
            s = s_buf[ib * n_pairs + j]
            m = jnp.max(s, axis=-1, keepdims=True)
            if has_sink:
                sk = jnp.where(top_rows, sink_ref[2 * j], sink_ref[2 * j + 1]) * LOG2E
                m = jnp.maximum(m, sk)
            e = jnp.exp2(s - m)
            ox = jnp.dot(e.astype(BF16), jnp.concatenate([v2, ones], axis=1),
                         preferred_element_type=F32)
            ov, den = ox[:, :LANES], ox[:, LANES:]
            if has_sink:
                ov = ov / (den + jnp.exp2(sk - m))
            o_dst[pl.ds(r0, BLK), j * LANES:(j + 1) * LANES] = (
                jnp.where(even, ov[:BLK], ov[BLK:]).astype(o_dst.dtype))
            if lse_dst is not None:
                for hh, half in ((2 * j, slice(0, BLK)), (2 * j + 1, slice(BLK, 2 * BLK))):
                    lse_tile = jnp.where(lane == hh, m[half], lse_tile)
                    lse_tile = jnp.where(lane == HEAD_DIM + hh, den[half], lse_tile)
        if lse_dst is not None:
            lse_dst[pl.ds(r0, BLK), :] = lse_tile

    if pieces > 1:
        for pc in range(pieces):
            for ib in range(nblk):
                dst = (ib * pieces + pc) * per
                o_ref[pc, ib * per:(ib + 1) * per] = o_dst[dst:dst + per]
                lse_ref[pc, ib * per:(ib + 1) * per] = lse_dst[dst:dst + per]


def _banded_attention(q, k, v, bias, pairs_per_kv, layout, sinks=None):
    wq, wkv = q.shape[-1], k.shape[-1]
    bn = q.shape[0]
    n_pairs = wq // LANES
    has_sink = sinks is not None
    if layout == "natural":
        seq = q.shape[1]
        nblk = min(BAND_BLOCKS, seq // BLK)
        rows = nblk * BLK
        grid = (bn, 1, seq // rows)
        cur = lambda b, r, n: (b, n, 0)
        prev = lambda b, r, n: (b, jnp.maximum(n * nblk - 1, 0), 0)
        cur_blk = lambda w: (None, rows, w)
        prev_blk = lambda w: (None, BLK, w)
        pieces = 1
    elif layout == "classes":
        n_cls, seq = q.shape[1], q.shape[2]
        nblk = min(BAND_BLOCKS, seq // BLK)
        rows = nblk * BLK
        grid = (bn, n_cls, seq // rows)
        cur = lambda b, r, n: (b, r, n, 0)
        prev = lambda b, r, n: (b, r, jnp.maximum(n * nblk - 1, 0), 0)
        cur_blk = lambda w: (None, None, rows, w)
        prev_blk = lambda w: (None, None, BLK, w)
        pieces = 1
    else:
        n_cls, seq = q.shape[1], q.shape[2]
        pieces = BAND_PIECES
        per = BLK // pieces
        nblk = pieces
        rows = BLK
        grid = (bn, n_cls // pieces, seq // rows)
        cur = lambda b, r, n: (b, r, n, 0)
        prev = lambda b, r, n: (b, r, jnp.maximum(n * (rows // per) - 1, 0), 0)
        cur_blk = lambda w: (None, pieces, rows, w)
        prev_blk = lambda w: (None, pieces, per, w)
    tot = nblk * BLK
    in_specs = [pl.BlockSpec(cur_blk(wq), cur),
                pl.BlockSpec(prev_blk(wkv), prev),
                pl.BlockSpec(cur_blk(wkv), cur),
                pl.BlockSpec(prev_blk(wkv), prev),
                pl.BlockSpec(cur_blk(wkv), cur),
                pl.BlockSpec(bias.shape, lambda b, r, n: (0, 0, 0))]
    args = [q, k, k, v, v, bias]
    out_specs = [pl.BlockSpec(cur_blk(wq), cur)]
    out_shape = [jax.ShapeDtypeStruct(q.shape, BF16)]
    scratch = [pltpu.VMEM((tot + BLK, wkv), BF16), pltpu.VMEM((tot + BLK, wkv), BF16),
               pltpu.VMEM((nblk * n_pairs, 2 * BLK, 2 * BLK), F32)]
    if has_sink:
        in_specs.insert(0, pl.BlockSpec(memory_space=pltpu.SMEM))
        args.insert(0, sinks)
    else:
        out_specs.append(pl.BlockSpec(cur_blk(LANES), cur))
        out_shape.append(jax.ShapeDtypeStruct(q.shape[:-1] + (LANES,), F32))
    if pieces > 1:
        scratch += [pltpu.VMEM((tot, wq), BF16), pltpu.VMEM((tot, wq), BF16),
                    pltpu.VMEM((tot, LANES), F32)]
    outs = pl.pallas_call(
        functools.partial(_banded_kernel, n_pairs=n_pairs, pairs_per_kv=pairs_per_kv,
                          has_sink=has_sink, pieces=pieces),
        grid=grid,
        in_specs=in_specs,
        out_specs=out_specs,
        out_shape=out_shape,
        scratch_shapes=scratch,
        compiler_params=pltpu.CompilerParams(
            dimension_semantics=("parallel", "parallel", "arbitrary"),
            vmem_limit_bytes=VMEM_LIMIT),
        name="banded_sink" if has_sink else f"banded_{layout}",
    )(*args)
    return outs[0] if has_sink else tuple(outs)


def _diff_kernel(far_ref, q_ref, kt_ref, v_ref, bias_ref, lam_ref, g_ref, o_ref,
                 qs_s, vx_s, m_s, acc_s, s_s, *, lam_init, n_near):
    h = pl.program_id(0)
    i = pl.program_id(2)
    t = q_ref.shape[0]
    n_rep = t // LANES

    @pl.when(i == 0)
    def _():
        vx_s[:, :LANES] = v_ref[...]
        vx_s[:, LANES:] = jnp.ones((vx_s.shape[0], LANES), BF16)

    _, m_even, m_odd = _lane_masks()
    q2 = q_ref[...]
    qs_s[:t] = q2 * m_even
    qs_s[t:] = q2 * m_odd
    m_s[...] = jnp.full(m_s.shape, NEG, F32)
    acc_s[...] = jnp.zeros(acc_s.shape, F32)
    far_bias = far_ref[h] * LOG2E

    def sweep(j0, n_tiles, near):
        k0 = pl.multiple_of(j0 * t, t)
        kts = [kt_ref[j0 + a] for a in range(n_tiles)]
        vx = vx_s[pl.ds(k0, n_tiles * t), :]
        width = n_tiles * t
        for r0 in range(0, 2 * t, DIFF_ROWS):
            rows = pl.ds(r0, DIFF_ROWS)
            qc = qs_s[rows, :]
            for a, kt in enumerate(kts):
                part = jnp.dot(qc, kt, preferred_element_type=F32)
                if near:
                    part = part + bias_ref[jnp.minimum(i - j0 - a, n_near),
                                           pl.ds(r0 % t, DIFF_ROWS), :]
                s_s[rows, a * t:(a + 1) * t] = part
        shift = 0.0 if near else far_bias
        for r0 in range(0, 2 * t, DIFF_ROWS):
            rows = pl.ds(r0, DIFF_ROWS)
            s = s_s[rows, :width]
            m_cur = jnp.max(s, axis=-1, keepdims=True) + shift
            m_old = m_s[rows, :]
            m_new = jnp.maximum(m_old, m_cur)
            alpha = jnp.exp2(m_old - m_new)
            m_sub = m_new - shift
            p = jnp.exp2(s - jnp.concatenate([m_sub] * (n_tiles * n_rep), axis=1))
            pv = jnp.dot(p.astype(BF16), vx, preferred_element_type=F32)
            acc_s[rows, :] = jnp.concatenate([alpha, alpha], axis=1) * acc_s[rows, :] + pv
            m_s[rows, :] = m_new

    n_far = jnp.maximum(i - (n_near - 1), 0)
    n_quads = n_far // DIFF_MAX_TILES
    j_pairs = DIFF_MAX_TILES * n_quads
    n_far_pairs = (n_far - j_pairs) >> 1
    j_near0 = j_pairs + 2 * n_far_pairs
    n_near_quads = jnp.where(i + 1 - j_near0 >= DIFF_MAX_TILES, 1, 0)
    j_near = j_near0 + DIFF_MAX_TILES * n_near_quads
    odd = (i + 1 - j_near) & 1

    def far_quad(jj, carry):
        sweep(DIFF_MAX_TILES * jj, DIFF_MAX_TILES, False)
        return carry

    def far_pair(jj, carry):
        sweep(j_pairs + 2 * jj, 2, False)
        return carry

    def near_quad(jj, carry):
        sweep(j_near0, DIFF_MAX_TILES, True)
        return carry

    lax.fori_loop(0, n_quads, far_quad, 0)
    lax.fori_loop(0, n_far_pairs, far_pair, 0)
    lax.fori_loop(0, n_near_quads, near_quad, 0)

    @pl.when(odd == 1)
    def _():
        sweep(j_near, 1, True)

    def near_pair(kk, carry):
        sweep(j_near + odd + 2 * kk, 2, True)
        return carry

    lax.fori_loop(0, (i + 1 - j_near) >> 1, near_pair, 0)

    acc = acc_s[...]
    o = acc[:, :LANES] / acc[:, LANES:]
    lam = (jnp.exp(jnp.sum(lam_ref[0:1, :] * lam_ref[1:2, :], axis=-1, keepdims=True))
           - jnp.exp(jnp.sum(lam_ref[2:3, :] * lam_ref[3:4, :], axis=-1, keepdims=True)) + lam_init)
    y = o[:t] - lam * o[t:]
    y = y * lax.rsqrt(jnp.mean(y * y, axis=-1, keepdims=True) + EPS)
    o_ref[...] = (y * g_ref[...] * (1.0 - lam_init)).astype(o_ref.dtype)


def _diff_attention(q, kt, v, bias_c, far_bias, lam_params, g_sub_l, lam_init):
    bn, s, w = q.shape
    t = TILE_M
    n_t = s // t
    n_heads = w // LANES
    n_near = bias_c.shape[1] - 1
    return pl.pallas_call(
        functools.partial(_diff_kernel, lam_init=lam_init, n_near=n_near),
        grid=(n_heads, bn, n_t),
        in_specs=[pl.BlockSpec(memory_space=pltpu.SMEM),
                  pl.BlockSpec((None, t, LANES), lambda h, b, i: (b, i, h)),
                  pl.BlockSpec((None, n_t, LANES, t), lambda h, b, i: (b, 0, h, 0)),
                  pl.BlockSpec((None, s, LANES), lambda h, b, i: (b, 0, h)),
                  pl.BlockSpec((None, n_near + 1, t, t), lambda h, b, i: (h, 0, 0, 0)),
                  pl.BlockSpec((4, HEAD_DIM), lambda h, b, i: (0, 0)),
                  pl.BlockSpec((1, LANES), lambda h, b, i: (0, 0))],
        out_specs=pl.BlockSpec((None, t, LANES), lambda h, b, i: (b, i, h)),
        out_shape=jax.ShapeDtypeStruct((bn, s, w), BF16),
        scratch_shapes=[pltpu.VMEM((2 * t, LANES), BF16),
                        pltpu.VMEM((s, 2 * LANES), BF16),
                        pltpu.VMEM((2 * t, LANES), F32),
                        pltpu.VMEM((2 * t, 2 * LANES), F32),
                        pltpu.VMEM((2 * t, DIFF_MAX_TILES * t), F32)],
        compiler_params=pltpu.CompilerParams(
            dimension_semantics=("parallel", "parallel", "arbitrary"),
            vmem_limit_bytes=VMEM_LIMIT),
        name="diff_attn",
    )(far_bias, q, kt, v, bias_c, lam_params, g_sub_l.reshape(1, LANES))


def _out_proj_kernel(x_ref, mod_ref, g_ref, w_ref, e_ref, unperm_ref, ya_ref, o1_ref, o4_ref,
                     o16_ref, l1_ref, l4_ref, l16_ref, yc_ref, z_ref, out_ref, l4_scr, l16_scr):
    d_model = x_ref.shape[-1]
    wa = ya_ref.shape[-1]
    wb = o1_ref.shape[-1]
    z = z_ref[...].astype(F32)
    sz = z / (1.0 + jnp.exp(-z))

    def to_natural(cm_ref, scr):
        n_cls, per_cls = cm_ref.shape[0], cm_ref.shape[1]
        for k in range(scr.shape[0]):
            for c in range(n_cls):
                scr[k, pl.ds(_class_residue(c), per_cls, stride=n_cls), :] = cm_ref[
                    c, :, k * LANES:(k + 1) * LANES].astype(F32)
        return jnp.concatenate([scr[k] for k in range(scr.shape[0])], axis=1)

    def rows_to_natural(cm_ref):
        n_cls, per_cls, w = cm_ref.shape
        return jnp.dot(unperm_ref[...], cm_ref[...].reshape(n_cls * per_cls, w),
                       preferred_element_type=F32)

    o_parts = [o1_ref[...].astype(F32), rows_to_natural(o4_ref), rows_to_natural(o16_ref)]
    stats = [l1_ref[...], to_natural(l4_ref, l4_scr), to_natural(l16_ref, l16_scr)]
    head_lane = lax.broadcasted_iota(jnp.int32, stats[0].shape, 1) < wb // HEAD_DIM
    maxes = [jnp.where(head_lane, st, 0.0) for st in stats]
    dens = [jnp.where(head_lane, pltpu.roll(st, HEAD_DIM, 1), 1.0) for st in stats]
    lses = [mxp + jnp.log2(dn) for mxp, dn in zip(maxes, dens)]
    mx = jnp.maximum(jnp.maximum(lses[0], lses[1]), lses[2])
    ws = [jnp.exp2(l - mx) for l in lses]
    tot = ws[0] + ws[1] + ws[2]
    yb = jnp.zeros((x_ref.shape[0], wb), F32)
    for wgt, dn, o_part in zip(ws, dens, o_parts):
        spread = jnp.dot(jnp.concatenate(_split_bf16(wgt / (tot * dn)), axis=1), e_ref[...],
                         preferred_element_type=F32)
        yb = yb + spread * o_part

    ga = (ya_ref[...].astype(F32) * sz[:, :wa]).astype(BF16)
    gb = (yb * sz[:, wa:wa + wb]).astype(BF16)
    gc = (yc_ref[...].astype(F32) * sz[:, wa + wb:]).astype(BF16)
    y = jnp.dot(ga, w_ref[:wa], preferred_element_type=F32)
    y += jnp.dot(gb, w_ref[wa:wa + wb], preferred_element_type=F32)
    y += jnp.dot(gc, w_ref[wa + wb:], preferred_element_type=F32)
    r = (y * lax.rsqrt(jnp.mean(y * y, axis=-1, keepdims=True) + EPS)) * g_ref[...]
    gate = mod_ref[:, 2 * d_model:]
    out_ref[...] = x_ref[...] + gate * r


def _out_proj(x, mod_l, g_post_l, w_out_l, spread, ya, ob, lb, yc, z):
    bn, s, d_model = x.shape
    tm = TILE_M
    per_cls = tm // N_CLASSES
    wb = ob[0].shape[-1]
    row = lambda w: pl.BlockSpec((None, tm, w), lambda b, i: (b, i, 0))
    cm = lambda w: pl.BlockSpec((None, N_CLASSES, per_cls, w), lambda b, i: (b, 0, i, 0))
    whole = lambda a: pl.BlockSpec(a.shape, lambda b, i: (0,) * a.ndim,
                                   pipeline_mode=pl.Buffered(1))
    g2 = g_post_l.reshape(1, d_model)
    unperm = _class_major_permutation(tm).T
    return pl.pallas_call(
        _out_proj_kernel,
        grid=(bn, s // tm),
        in_specs=[row(d_model),
                  pl.BlockSpec((None, 1, 3 * d_model), lambda b, i: (b, 0, 0)),
                  whole(g2), whole(w_out_l), whole(spread), whole(unperm),
                  row(ya.shape[-1]),
                  row(wb), cm(wb), cm(wb),
                  row(LANES), cm(LANES), cm(LANES),
                  row(yc.shape[-1]), row(z.shape[-1])],
        out_specs=row(d_model),
        out_shape=jax.ShapeDtypeStruct(x.shape, x.dtype),
        scratch_shapes=[pltpu.VMEM((1, tm, LANES), F32), pltpu.VMEM((1, tm, LANES), F32)],
        compiler_params=pltpu.CompilerParams(
            dimension_semantics=("parallel", "parallel"), vmem_limit_bytes=VMEM_LIMIT),
        name="out_proj",
    )(x, mod_l.reshape(bn, 1, 3 * d_model), g2, w_out_l, spread, unperm, ya, *ob, *lb, yc, z)


def _prepare_w_in(w, sizes, n_kv_a):
    offs = np.concatenate([[0], np.cumsum(sizes)])
    grp = lambda i: w[:, int(offs[i]):int(offs[i + 1])]
    aq, ak, av, bq, bk, bv, cq, ck, cv, z = [grp(i) for i in range(10)]

    def dup(m):
        parts = []
        for g in range(n_kv_a):
            head = m[:, g * HEAD_DIM:(g + 1) * HEAD_DIM]
            parts += [head, head]
        return jnp.concatenate(parts, axis=1)

    groups = [aq * Q_SCALE, dup(ak), dup(av), bq * Q_SCALE, bk, bv, cq * Q_SCALE, cv, z]
    widths = tuple(int(g.shape[1]) for g in groups)
    w_cols = jnp.concatenate(groups, axis=1).astype(BF16)
    return w_cols, ck.T.astype(BF16), widths


def kernel(x, c, rel_table, w_in, w_out, w_ada, b_ada, g_pre, g_post, a_sinks,
           lam_q1, lam_k1, lam_q2, lam_k2, g_sub):
    bn, s, d_model = x.shape
    depth = w_in.shape[0]
    d_mix = w_out.shape[1]
    a_width = b_width = 3 * d_mix // 8
    c_width = d_mix // 4
    n_a_heads = a_width // HEAD_DIM
    n_kv_a = n_a_heads // 4
    n_b_heads = b_width // HEAD_DIM
    n_c_heads = c_width // (2 * HEAD_DIM)
    sizes = (a_width, n_kv_a * HEAD_DIM, n_kv_a * HEAD_DIM, b_width, b_width, b_width,
             c_width, c_width, c_width, d_mix)
    assert w_in.shape[2] == sum(sizes) and rel_table.shape == (REL_BUCKETS,
                                                                n_a_heads + n_b_heads + n_c_heads)
    assert n_a_heads == n_b_heads and s % (N_CLASSES * BLK) == 0
    assert [d for _, d in B_PATTERNS] == [1, BAND_PIECES, N_CLASSES]
    assert s % TILE_M == 0 and TILE_M % BLK == 0

    n_near = -(-(FAR_DIST - 1) // TILE_M) + 1
    bias_a, bias_b, bias_c = _bias_tables(rel_table, n_a_heads, n_b_heads, n_c_heads,
                                          TILE_M, n_near + 1)
    far_bias = rel_table[REL_BUCKETS - 1, n_a_heads + n_b_heads:]
    mod = _modulation(c, w_ada, b_ada)

    head_of_lane = np.arange(b_width) // HEAD_DIM
    spread = jnp.asarray(np.arange(2 * LANES)[:, None] % LANES == head_of_lane[None, :], BF16)

    for l in range(depth):
        w_cols, w_kt, widths = _prepare_w_in(w_in[l], sizes, n_kv_a)
        (aq, ak, av, bq, bk, bv, cq, cv, z, bq_cm, bk_cm, bv_cm, ckt) = _in_proj(
            x, mod[l], g_pre[l], w_cols, w_kt, widths, cm_groups=(3, 4, 5))

        ya = _banded_attention(aq, ak, av, bias_a, 2, "natural", sinks=a_sinks[l])
        o1, l1 = _banded_attention(bq, bk, bv, bias_b[0], 1, "natural")
        o4, l4 = _banded_attention(bq_cm, bk_cm, bv_cm, bias_b[1], 1, "pieces")
        o16, l16 = _banded_attention(bq_cm, bk_cm, bv_cm, bias_b[2], 1, "classes")
        ob, lb = (o1, o4, o16), (l1, l4, l16)

        lam_init = 0.8 - 0.6 * math.exp(-0.3 * l)
        lam_params = jnp.stack([lam_q1[l], lam_k1[l], lam_q2[l], lam_k2[l]]).astype(F32)
        yc = _diff_attention(cq, ckt, cv, bias_c, far_bias, lam_params, g_sub[l], lam_init)

        x = _out_proj(x, mod[l], g_post[l], w_out[l].astype(BF16), spread, ya, ob, lb, yc, z)
    return x
```
